```python
import jax, jax.numpy as jnp
from jax import lax
import numpy as np

D_MODEL = 1024
BATCH = 8
SEQ = 2048
DEPTH = 4

CHUNK = 64
Q_BLOCK = 128
N_EVEN = (DEPTH + 1) // 2
N_ODD = DEPTH // 2

CONV_CH = D_MODEL // 2
CONV_WIDTH = 31
DIFF_HEADS = 4
DIFF_HEAD_DIM = 64
DIFF_V_DIM = 2 * DIFF_HEAD_DIM
DIFF_WIDTH = DIFF_HEADS * DIFF_V_DIM
IN_COLS = 2 * CONV_CH + 3 * DIFF_WIDTH
MIX_WIDTH = CONV_CH + DIFF_WIDTH
ROPE_THETA = 500000.0
ROT_DIM = DIFF_HEAD_DIM // 4

POOL_WINDOWS = (2, 4, 8, 16)
POOL_GROUPS = len(POOL_WINDOWS)
POOL_GROUP_CH = D_MODEL // POOL_GROUPS

D_FF = 2816
N_SUB = 3
EPS = 1e-6

kernel_name = "hybrid_conv_diffattn_pool_macaron_trunk"


def rmsnorm(x, g, eps=EPS):
    xf = x.astype(jnp.float32)
    y = xf * lax.rsqrt(jnp.mean(xf * xf, axis=-1, keepdims=True) + eps)
    return (y * g.astype(jnp.float32)).astype(x.dtype)


def layernorm(x, g, b, eps=EPS):
    xf = x.astype(jnp.float32)
    mu = jnp.mean(xf, axis=-1, keepdims=True)
    var = jnp.mean(jnp.square(xf - mu), axis=-1, keepdims=True)
    y = (xf - mu) * lax.rsqrt(var + eps)
    return (y * g.astype(jnp.float32) + b.astype(jnp.float32)).astype(x.dtype)


def modulate(x, g, shift, scale):
    return rmsnorm(x, g) * (1.0 + scale[:, None, :]) + shift[:, None, :]


def swiglu(h, w1, w3, w2):
    return (jax.nn.silu(h @ w1) * (h @ w3)) @ w2


def rotary_tables(positions):
    inv_freq = ROPE_THETA ** (-jnp.arange(0, ROT_DIM, 2, dtype=jnp.float32) / ROT_DIM)
    ang = positions.astype(jnp.float32)[..., None] * inv_freq
    return jnp.cos(ang), jnp.sin(ang)


def partial_rotary(t, cos, sin):
    half = ROT_DIM // 2
    cos = cos[:, :, None, None, :].astype(t.dtype)
    sin = sin[:, :, None, None, :].astype(t.dtype)
    t1 = t[..., :half]
    t2 = t[..., half:ROT_DIM]
    return jnp.concatenate([t1 * cos - t2 * sin, t2 * cos + t1 * sin, t[..., ROT_DIM:]], axis=-1)


def diff_attention(q, k, v, lam):
    S = q.shape[1]
    scale = DIFF_HEAD_DIM ** -0.5
    outs = []
    for i in range(S // Q_BLOCK):
        q0 = i * Q_BLOCK
        kend = q0 + Q_BLOCK
        s = jnp.einsum('bqhcd,bkhcd->bhcqk', q[:, q0:kend], k[:, :kend],
                       preferred_element_type=jnp.float32) * scale
        q_chunk = jnp.arange(q0, kend) // CHUNK
        k_chunk = jnp.arange(kend) // CHUNK
        mask = q_chunk[:, None] >= k_chunk[None, :]
        p = jax.nn.softmax(jnp.where(mask, s, -jnp.inf), axis=-1)
        w = p[:, :, 0] - lam * p[:, :, 1]
        outs.append(jnp.einsum('bhqk,bkhe->bqhe', w.astype(v.dtype), v[:, :kend]))
    return jnp.concatenate(outs, axis=1)


def conv_diff_mixer(h, cos, sin, w_in, w_out, conv_w, conv_b, conv_ln_g, conv_ln_b,
                    diff_lambda, subln_g, lambda_init):
    B, S, _ = h.shape
    z = h @ w_in
    a_val, a_gate, q, k, v = jnp.split(
        z, [CONV_CH, 2 * CONV_CH, 2 * CONV_CH + DIFF_WIDTH, 2 * CONV_CH + 2 * DIFF_WIDTH], axis=-1)
    a = a_val * jax.nn.sigmoid(a_gate)
    a = lax.conv_general_dilated(a, conv_w[:, None, :], window_strides=(1,),
                                 padding=[(CONV_WIDTH - 1, 0)],
                                 dimension_numbers=('NWC', 'WIO', 'NWC'),
                                 feature_group_count=CONV_CH) + conv_b
    a = jax.nn.silu(layernorm(a, conv_ln_g, conv_ln_b))
    q = partial_rotary(q.reshape(B, S, DIFF_HEADS, 2, DIFF_HEAD_DIM), cos, sin)
    k = partial_rotary(k.reshape(B, S, DIFF_HEADS, 2, DIFF_HEAD_DIM), cos, sin)
    v = v.reshape(B, S, DIFF_HEADS, DIFF_V_DIM)
    dl = diff_lambda.astype(jnp.float32)
    lam = jnp.exp(jnp.sum(dl[0] * dl[1])) - jnp.exp(jnp.sum(dl[2] * dl[3])) + lambda_init
    o = diff_attention(q, k, v, lam)
    o = (rmsnorm(o, subln_g) * (1.0 - lambda_init)).reshape(B, S, DIFF_WIDTH)
    return jnp.concatenate([a, o], axis=-1) @ w_out


def multiscale_pool(h, pool_w, pool_scale):
    B, S, D = h.shape
    hf = h.astype(jnp.float32)
    cs = lax.cumsum(hf, axis=1)
    count_base = jnp.arange(1, S + 1, dtype=jnp.float32)
    outs = []
    for g, win in enumerate(POOL_WINDOWS):
        lo, hi = g * POOL_GROUP_CH, (g + 1) * POOL_GROUP_CH
        cg = cs[..., lo:hi]
        lag = jnp.pad(cg[:, :S - win], ((0, 0), (win, 0), (0, 0)))
        mean = (cg - lag) / jnp.minimum(count_base, win)[None, :, None]
        outs.append((mean - hf[..., lo:hi]).astype(h.dtype))
    p = jnp.stack(outs, axis=2)
    y = jnp.einsum('bsgc,gce->bsge', p, pool_w).reshape(B, S, D)
    return y * pool_scale


def setup_inputs(seed: int = 0) -> dict:
    key = jax.random.key(seed)
    ks = jax.random.split(key, 24)
    f32 = jnp.float32
    nrm = lambda k, shape, s: jax.random.normal(k, shape, f32) * s
    x = jax.random.normal(ks[0], (BATCH, SEQ, D_MODEL), f32)
    c = jax.random.normal(ks[1], (BATCH, D_MODEL), f32)
    offset = jax.random.randint(ks[2], (BATCH, 1), 0, 64, dtype=jnp.int32) * CHUNK
    positions = offset + jnp.arange(SEQ, dtype=jnp.int32)[None, :]
    return {
        "x": x,
        "c": c,
        "positions": positions,
        "w_ada": nrm(ks[3], (DEPTH, D_MODEL, N_SUB * 3 * D_MODEL), 0.5 * D_MODEL ** -0.5),
        "b_ada": nrm(ks[4], (DEPTH, N_SUB * 3 * D_MODEL), 0.01),
        "norm_g": 1.0 + nrm(ks[5], (DEPTH, N_SUB, D_MODEL), 0.02),
        "ffn_w1": nrm(ks[6], (DEPTH, 2, D_MODEL, D_FF), D_MODEL ** -0.5),
        "ffn_w3": nrm(ks[7], (DEPTH, 2, D_MODEL, D_FF), D_MODEL ** -0.5),
        "ffn_w2": nrm(ks[8], (DEPTH, 2, D_FF, D_MODEL), D_FF ** -0.5),
        "w_in": nrm(ks[9], (N_EVEN, D_MODEL, IN_COLS), D_MODEL ** -0.5),
        "w_out": nrm(ks[10], (N_EVEN, MIX_WIDTH, D_MODEL), MIX_WIDTH ** -0.5),
        "conv_w": nrm(ks[11], (N_EVEN, CONV_WIDTH, CONV_CH), CONV_WIDTH ** -0.5),
        "conv_b": nrm(ks[12], (N_EVEN, CONV_CH), 0.01),
        "conv_ln_g": 1.0 + nrm(ks[13], (N_EVEN, CONV_CH), 0.02),
        "conv_ln_b": nrm(ks[14], (N_EVEN, CONV_CH), 0.01),
        "diff_lambda": nrm(ks[15], (N_EVEN, 4, DIFF_HEAD_DIM), 0.1),
        "subln_g": 1.0 + nrm(ks[16], (N_EVEN, DIFF_V_DIM), 0.02),
        "pool_w": nrm(ks[17], (N_ODD, POOL_GROUPS, POOL_GROUP_CH, POOL_GROUP_CH), POOL_GROUP_CH ** -0.5),
        "pool_scale": 1.0 + nrm(ks[18], (N_ODD, D_MODEL), 0.1),
        "final_g": 1.0 + nrm(ks[19], (D_MODEL,), 0.02),
    }


def reference(x, c, positions, w_ada, b_ada, norm_g, ffn_w1, ffn_w3, ffn_w2, w_in, w_out,
              conv_w, conv_b, conv_ln_g, conv_ln_b, diff_lambda, subln_g, pool_w, pool_scale,
              final_g):
    B = x.shape[0]
    cos, sin = rotary_tables(positions)
    cond = jax.nn.silu(c)
    for l in range(DEPTH):
        mod = (cond @ w_ada[l] + b_ada[l]).reshape(B, N_SUB, 3, D_MODEL)
        shift, scale, gate = mod[:, :, 0], mod[:, :, 1], mod[:, :, 2]
        h = modulate(x, norm_g[l, 0], shift[:, 0], scale[:, 0])
        x = x + 0.5 * gate[:, 0, None, :] * swiglu(h, ffn_w1[l, 0], ffn_w3[l, 0], ffn_w2[l, 0])
        h = modulate(x, norm_g[l, 1], shift[:, 1], scale[:, 1])
        if l % 2 == 0:
            e = l // 2
            lambda_init = 0.8 - 0.6 * float(np.exp(-0.3 * l))
            y = conv_diff_mixer(h, cos, sin, w_in[e], w_out[e], conv_w[e], conv_b[e],
                                conv_ln_g[e], conv_ln_b[e], diff_lambda[e], subln_g[e],
                                lambda_init)
        else:
            o = l // 2
            y = multiscale_pool(h, pool_w[o], pool_scale[o])
        x = x + gate[:, 1, None, :] * y
        h = modulate(x, norm_g[l, 2], shift[:, 2], scale[:, 2])
        x = x + 0.5 * gate[:, 2, None, :] * swiglu(h, ffn_w1[l, 1], ffn_w3[l, 1], ffn_w2[l, 1])
    return rmsnorm(x, final_g)
```

```python
import functools

import numpy as np
import jax
import jax.numpy as jnp
from jax import lax
from jax.experimental import pallas as pl
from jax.experimental.pallas import tpu as pltpu

D_MODEL = 1024
BATCH = 8
SEQ = 2048
DEPTH = 4
CHUNK = 64
CONV_CH = D_MODEL // 2
CONV_WIDTH = 31
DIFF_HEADS = 4
DIFF_HEAD_DIM = 64
DIFF_V_DIM = 2 * DIFF_HEAD_DIM
DIFF_WIDTH = DIFF_HEADS * DIFF_V_DIM
IN_COLS = 2 * CONV_CH + 3 * DIFF_WIDTH
MIX_WIDTH = CONV_CH + DIFF_WIDTH
ROPE_THETA = 500000.0
ROT_DIM = DIFF_HEAD_DIM // 4
POOL_WINDOWS = (2, 4, 8, 16)
POOL_GROUP_CH = D_MODEL // len(POOL_WINDOWS)
D_FF = 2816
N_SUB = 3
EPS = 1e-6

TOKENS = BATCH * SEQ
MOD_ROWS_PER_BATCH = N_SUB * 3
MOD_ROWS_PER_LAYER = BATCH * MOD_ROWS_PER_BATCH

V7X_VMEM_LIMIT_BYTES = 56 * 1024 * 1024
LANES = 128
HALO = 32
POOL_HALO = 16

TM_FFN = 512
TM_MIX = 512
ADA_TN = 1536
ATT_CLASS = 512
ATT_QBLK = 128

F32 = jnp.float32
BF16 = jnp.bfloat16


def _silu(v):
    return v * (1.0 / (1.0 + jnp.exp(-v)))


def _sigmoid(v):
    return 1.0 / (1.0 + jnp.exp(-v))


def _modulate(x, g, shift, scale):
    ms = jnp.mean(x * x, axis=-1, keepdims=True)
    return (x * lax.rsqrt(ms + EPS) * g) * (1.0 + scale) + shift


def _mod_spec(layer, sub, which, tiles_per_batch):
    base = layer * MOD_ROWS_PER_LAYER + sub * 3 + which
    return pl.BlockSpec(
        (None, 1, D_MODEL),
        lambda i: (base + (i // tiles_per_batch) * MOD_ROWS_PER_BATCH, 0, 0))


def _const_spec(shape):
    nd = len(shape)
    return pl.BlockSpec(shape, lambda i: (0,) * nd, pipeline_mode=pl.Buffered(1))


def _params(sem):
    return pltpu.CompilerParams(dimension_semantics=sem, vmem_limit_bytes=V7X_VMEM_LIMIT_BYTES)


def _ada_kernel(c_ref, w_ref, b_ref, o_ref):
    cond = _silu(c_ref[...]).astype(BF16)
    o_ref[...] = jnp.dot(cond, w_ref[...].astype(BF16), preferred_element_type=F32) + b_ref[...]


def _ada_call(c, w_ada, b_ada):
    n = N_SUB * 3 * D_MODEL
    return pl.pallas_call(
        _ada_kernel,
        grid=(DEPTH, n // ADA_TN),
        in_specs=[
            pl.BlockSpec((BATCH, D_MODEL), lambda l, j: (0, 0)),
            pl.BlockSpec((None, D_MODEL, ADA_TN), lambda l, j: (l, 0, j)),
            pl.BlockSpec((None, 1, ADA_TN), lambda l, j: (l, 0, j)),
        ],
        out_specs=pl.BlockSpec((None, BATCH, ADA_TN), lambda l, j: (l, 0, j)),
        out_shape=jax.ShapeDtypeStruct((DEPTH, BATCH, n), F32),
        compiler_params=_params(("arbitrary", "arbitrary")),
        name="ada_mod",
    )(c, w_ada, b_ada.reshape(DEPTH, 1, n))


def _ffn_kernel(x_ref, shift_ref, scale_ref, gate_ref, g_ref, w1_ref, w3_ref, w2_ref, fg_ref,
                o_ref, *, final):
    x = x_ref[...]
    h = _modulate(x, g_ref[...], shift_ref[...], scale_ref[...]).astype(BF16)
    a = jnp.dot(h, w1_ref[...], preferred_element_type=F32)
    b = jnp.dot(h, w3_ref[...], preferred_element_type=F32)
    u = (_silu(a) * b).astype(BF16)
    y = jnp.dot(u, w2_ref[...], preferred_element_type=F32)
    xo = x + (0.5 * gate_ref[...]) * y
    if final:
        ms = jnp.mean(xo * xo, axis=-1, keepdims=True)
        xo = xo * lax.rsqrt(ms + EPS) * fg_ref[...]
    o_ref[...] = xo


def _ffn_call(x, mods, g_row, w1, w3, w2, final_g, layer, sub, final):
    tpb = SEQ // TM_FFN
    return pl.pallas_call(
        functools.partial(_ffn_kernel, final=final),
        grid=(TOKENS // TM_FFN,),
        in_specs=[
            pl.BlockSpec((TM_FFN, D_MODEL), lambda i: (i, 0)),
            _mod_spec(layer, sub, 0, tpb),
            _mod_spec(layer, sub, 1, tpb),
            _mod_spec(layer, sub, 2, tpb),
            _const_spec((1, D_MODEL)),
            _const_spec((D_MODEL, D_FF)),
            _const_spec((D_MODEL, D_FF)),
            _const_spec((D_FF, D_MODEL)),
            _const_spec((1, D_MODEL)),
        ],
        out_specs=pl.BlockSpec((TM_FFN, D_MODEL), lambda i: (i, 0)),
        out_shape=jax.ShapeDtypeStruct((TOKENS, D_MODEL), F32),
        compiler_params=_params(("arbitrary",)),
        name=f"ffn_l{layer}_s{sub}",
    )(x, mods, mods, mods, g_row, w1, w3, w2, final_g)


def _mix_in_kernel(x_ref, shift_ref, scale_ref, g_ref, tab_ref, w_ref,
                   a_ref, q_ref, k_ref, v_ref):
    h = _modulate(x_ref[...], g_ref[...], shift_ref[...], scale_ref[...]).astype(BF16)
    z = jnp.dot(h, w_ref[...], preferred_element_type=F32)
    a_ref[...] = z[:, :CONV_CH] * _sigmoid(z[:, CONV_CH:2 * CONV_CH])
    cmul = tab_ref[:, 0:LANES]
    s_lo = tab_ref[:, LANES:2 * LANES]
    s_hi = tab_ref[:, 2 * LANES:3 * LANES]
    half = ROT_DIM // 2
    q_off = 2 * CONV_CH
    k_off = q_off + DIFF_WIDTH
    for hd in range(DIFF_HEADS):
        for off, ref, mul in ((q_off, q_ref, DIFF_HEAD_DIM ** -0.5), (k_off, k_ref, None)):
            t = z[:, off + hd * LANES: off + (hd + 1) * LANES]
            r = (t * cmul + pltpu.roll(t, LANES - half, 1) * s_lo + pltpu.roll(t, half, 1) * s_hi)
            if mul is not None:
                r = r * mul
            ref[:, hd * LANES:(hd + 1) * LANES] = r.astype(BF16)
    v_ref[...] = z[:, k_off + DIFF_WIDTH:].astype(BF16)


def _mix_in_call(x, mods, g_row, tab, w_in, layer):
    tpb = SEQ // TM_MIX
    tok = lambda w: pl.BlockSpec((TM_MIX, w), lambda i: (i, 0))
    return pl.pallas_call(
        _mix_in_kernel,
        grid=(TOKENS // TM_MIX,),
        in_specs=[
            tok(D_MODEL),
            _mod_spec(layer, 1, 0, tpb),
            _mod_spec(layer, 1, 1, tpb),
            _const_spec((1, D_MODEL)),
            tok(3 * LANES),
            _const_spec((D_MODEL, IN_COLS)),
        ],
        out_specs=[tok(CONV_CH), tok(DIFF_WIDTH), tok(DIFF_WIDTH), tok(DIFF_WIDTH)],
        out_shape=[
            jax.ShapeDtypeStruct((TOKENS, CONV_CH), F32),
            jax.ShapeDtypeStruct((TOKENS, DIFF_WIDTH), BF16),
            jax.ShapeDtypeStruct((TOKENS, DIFF_WIDTH), BF16),
            jax.ShapeDtypeStruct((TOKENS, DIFF_WIDTH), BF16),
        ],
        compiler_params=_params(("arbitrary",)),
        name=f"mix_in_l{layer}",
    )(x, mods, mods, g_row, tab, w_in)


def _attn_kernel(q_ref, k_ref, v_ref, dl_ref, sg_ref, o_ref, *, lambda_init):
    dl = dl_ref[...]
    lam = (jnp.exp(jnp.sum(dl[0:1] * dl[1:2], axis=1, keepdims=True))
           - jnp.exp(jnp.sum(dl[2:3] * dl[3:4], axis=1, keepdims=True)) + lambda_init)
    first = lax.broadcasted_iota(jnp.int32, (1, LANES), 1) < DIFF_HEAD_DIM
    out_gain = sg_ref[...] * (1.0 - lambda_init)
    nt = (((1,), (1,)), ((), ()))
    row_chunk = lax.broadcasted_iota(jnp.int32, (ATT_QBLK, 1), 0) // CHUNK
    col_chunk = lax.broadcasted_iota(jnp.int32, (1, ATT_CLASS), 1) // CHUNK

    for cls in range(SEQ // ATT_CLASS):
        klen = ATT_CLASS * (cls + 1)
        koff = klen - ATT_CLASS

        def body(r, carry, klen=klen, koff=koff, cls=cls):
            row0 = pl.multiple_of(cls * ATT_CLASS + r * ATT_QBLK, ATT_QBLK)
            q = q_ref[pl.ds(row0, ATT_QBLK), :]
            visible = col_chunk <= (row_chunk + r * (ATT_QBLK // CHUNK))
            parts = []
            for qc in (jnp.where(first, q, jnp.zeros_like(q)), jnp.where(first, jnp.zeros_like(q), q)):
                s_diag = lax.dot_general(qc, k_ref[koff:klen, :], nt, preferred_element_type=F32)
                s_diag = jnp.where(visible, s_diag, -jnp.inf)
                m = jnp.max(s_diag, axis=1, keepdims=True)
                if koff:
                    s_off = lax.dot_general(qc, k_ref[0:koff, :], nt, preferred_element_type=F32)
                    m = jnp.maximum(m, jnp.max(s_off, axis=1, keepdims=True))
                    p_off = jnp.exp(s_off - m)
                p_diag = jnp.exp(s_diag - m)
                den = jnp.sum(p_diag, axis=1, keepdims=True)
                if koff:
                    den = den + jnp.sum(p_off, axis=1, keepdims=True)
                    parts.append((p_diag, p_off, 1.0 / den))
                else:
                    parts.append((p_diag, None, 1.0 / den))
            (pd0, po0, r0), (pd1, po1, r1) = parts
            r1 = lam * r1
            w_diag = (pd0 * r0 - pd1 * r1).astype(BF16)
            o = jnp.dot(w_diag, v_ref[koff:klen, :], preferred_element_type=F32)
            if koff:
                w_off = (po0 * r0 - po1 * r1).astype(BF16)
                o = o + jnp.dot(w_off, v_ref[0:koff, :], preferred_element_type=F32)
            ms = jnp.mean(o * o, axis=-1, keepdims=True)
            o_ref[pl.ds(row0, ATT_QBLK), :] = (o * lax.rsqrt(ms + EPS) * out_gain).astype(BF16)
            return carry

        lax.fori_loop(0, ATT_CLASS // ATT_QBLK, body, 0)


def _attn_call(q, k, v, diff_lambda, subln_g, layer):
    lambda_init = 0.8 - 0.6 * float(np.exp(-0.3 * layer))
    head = lambda: pl.BlockSpec((None, SEQ, LANES), lambda b, h: (b, 0, h))
    shp = (BATCH, SEQ, DIFF_WIDTH)
    return pl.pallas_call(
        functools.partial(_attn_kernel, lambda_init=lambda_init),
        grid=(BATCH, DIFF_HEADS),
        in_specs=[
            head(), head(), head(),
            pl.BlockSpec((4, DIFF_HEAD_DIM), lambda b, h: (0, 0)),
            pl.BlockSpec((1, DIFF_V_DIM), lambda b, h: (0, 0)),
        ],
        out_specs=head(),
        out_shape=jax.ShapeDtypeStruct(shp, BF16),
        compiler_params=_params(("arbitrary", "arbitrary")),
        name=f"diff_attn_l{layer}",
    )(q.reshape(shp), k.reshape(shp), v.reshape(shp), diff_lambda, subln_g)


def _mix_out_kernel(x_ref, gate_ref, a_ref, halo_ref, o_ref, cw_ref, cb_ref, lg_ref, lb_ref, w_ref,
                    out_ref, abuf, conv):
    i = pl.program_id(0)
    seq_start = (i % (SEQ // TM_MIX)) == 0
    halo = halo_ref[...]
    abuf[0:HALO, :] = jnp.where(seq_start, jnp.zeros_like(halo), halo)
    abuf[HALO:, :] = a_ref[...]
    rb = 32
    lead = HALO - (CONV_WIDTH - 1)
    for blk in range(TM_MIX // rb):
        acc = jnp.broadcast_to(cb_ref[...], (rb, CONV_CH))
        for j in range(CONV_WIDTH):
            start = blk * rb + lead + j
            acc = acc + cw_ref[j:j + 1, :] * abuf[start:start + rb, :]
        conv[blk * rb:(blk + 1) * rb, :] = acc
    c = conv[...]
    mu = jnp.mean(c, axis=-1, keepdims=True)
    d = c - mu
    var = jnp.mean(d * d, axis=-1, keepdims=True)
    a = _silu(d * lax.rsqrt(var + EPS) * lg_ref[...] + lb_ref[...]).astype(BF16)
    y = (jnp.dot(a, w_ref[0:CONV_CH, :], preferred_element_type=F32)
         + jnp.dot(o_ref[...], w_ref[CONV_CH:, :], preferred_element_type=F32))
    out_ref[...] = x_ref[...] + gate_ref[...] * y


def _mix_out_call(x, mods, a, o, conv_w, conv_b, ln_g, ln_b, w_out, layer):
    tpb = SEQ // TM_MIX
    tok = lambda w: pl.BlockSpec((TM_MIX, w), lambda i: (i, 0))
    halo_blocks = TM_MIX // HALO
    return pl.pallas_call(
        _mix_out_kernel,
        grid=(TOKENS // TM_MIX,),
        in_specs=[
            tok(D_MODEL),
            _mod_spec(layer, 1, 2, tpb),
            tok(CONV_CH),
            pl.BlockSpec((HALO, CONV_CH), lambda i: (jnp.maximum(i * halo_blocks - 1, 0), 0)),
            tok(DIFF_WIDTH),
            _const_spec((CONV_WIDTH, CONV_CH)),
            _const_spec((1, CONV_CH)),
            _const_spec((1, CONV_CH)),
            _const_spec((1, CONV_CH)),
            _const_spec((MIX_WIDTH, D_MODEL)),
        ],
        out_specs=tok(D_MODEL),
        out_shape=jax.ShapeDtypeStruct((TOKENS, D_MODEL), F32),
        scratch_shapes=[pltpu.VMEM((HALO + TM_MIX, CONV_CH), F32),
                        pltpu.VMEM((TM_MIX, CONV_CH), F32)],
        compiler_params=_params(("arbitrary",)),
        name=f"mix_out_l{layer}",
    )(x, mods, a, a, o, conv_w, conv_b, ln_g, ln_b, w_out)


def _pool_kernel(x_ref, xh_ref, shift_ref, scale_ref, gate_ref, g_ref, pw_ref, ps_ref,
                 out_ref, hbuf):
    i = pl.program_id(0)
    t_in_seq = (i % (SEQ // TM_MIX)) * TM_MIX
    x = x_ref[...]
    g, shift, scale = g_ref[...], shift_ref[...], scale_ref[...]
    hh = _modulate(xh_ref[...], g, shift, scale)
    hbuf[0:POOL_HALO, :] = jnp.where(t_in_seq == 0, jnp.zeros_like(hh), hh)
    hbuf[POOL_HALO:, :] = _modulate(x, g, shift, scale)
    pos = (t_in_seq + 1 + lax.broadcasted_iota(jnp.int32, (TM_MIX, 1), 0)).astype(F32)
    ys = []
    for grp, win in enumerate(POOL_WINDOWS):
        lo, hi = grp * POOL_GROUP_CH, (grp + 1) * POOL_GROUP_CH
        cur = hbuf[POOL_HALO:, lo:hi]
        tot = cur
        for lag in range(1, win):
            tot = tot + hbuf[POOL_HALO - lag:POOL_HALO - lag + TM_MIX, lo:hi]
        p = (tot / jnp.minimum(pos, float(win)) - cur).astype(BF16)
        ys.append(jnp.dot(p, pw_ref[grp], preferred_element_type=F32))
    y = jnp.concatenate(ys, axis=-1) * ps_ref[...]
    out_ref[...] = x + gate_ref[...] * y


def _pool_call(x, mods, g_row, pool_w, pool_scale, layer):
    tpb = SEQ // TM_MIX
    tok = pl.BlockSpec((TM_MIX, D_MODEL), lambda i: (i, 0))
    halo_blocks = TM_MIX // POOL_HALO
    return pl.pallas_call(
        _pool_kernel,
        grid=(TOKENS // TM_MIX,),
        in_specs=[
            tok,
            pl.BlockSpec((POOL_HALO, D_MODEL), lambda i: (jnp.maximum(i * halo_blocks - 1, 0), 0)),
            _mod_spec(layer, 1, 0, tpb),
            _mod_spec(layer, 1, 1, tpb),
            _mod_spec(layer, 1, 2, tpb),
            _const_spec((1, D_MODEL)),
            _const_spec((len(POOL_WINDOWS), POOL_GROUP_CH, POOL_GROUP_CH)),
            _const_spec((1, D_MODEL)),
        ],
        out_specs=tok,
        out_shape=jax.ShapeDtypeStruct((TOKENS, D_MODEL), F32),
        scratch_shapes=[pltpu.VMEM((POOL_HALO + TM_MIX, D_MODEL), F32)],
        compiler_params=_params(("arbitrary",)),
        name=f"pool_l{layer}",
    )(x, x, mods, mods, mods, g_row, pool_w, pool_scale)


def _rotary_table(positions):
    half = ROT_DIM // 2
    inv_freq = ROPE_THETA ** (-jnp.arange(0, ROT_DIM, 2, dtype=F32) / ROT_DIM)
    ang = positions.astype(F32).reshape(TOKENS, 1) * inv_freq
    cos, sin = jnp.cos(ang), jnp.sin(ang)
    rest = DIFF_HEAD_DIM - ROT_DIM
    zeros = lambda n: jnp.zeros((TOKENS, n), F32)
    cmul = jnp.concatenate([cos, cos, jnp.ones((TOKENS, rest), F32)], axis=1)
    s_lo = jnp.concatenate([-sin, zeros(half + rest)], axis=1)
    s_hi = jnp.concatenate([zeros(half), sin, zeros(rest)], axis=1)
    return jnp.concatenate([cmul, cmul, s_lo, s_lo, s_hi, s_hi], axis=1)


def kernel(x, c, positions, w_ada, b_ada, norm_g, ffn_w1, ffn_w3, ffn_w2, w_in, w_out, conv_w, conv_b,
           conv_ln_g, conv_ln_b, diff_lambda, subln_g, pool_w, pool_scale, final_g):
    mods = _ada_call(c, w_ada, b_ada).reshape(DEPTH * MOD_ROWS_PER_LAYER, 1, D_MODEL)
    tab = _rotary_table(positions)
    w1b, w3b, w2b = ffn_w1.astype(BF16), ffn_w3.astype(BF16), ffn_w2.astype(BF16)
    w_in_b, w_out_b, pool_w_b = w_in.astype(BF16), w_out.astype(BF16), pool_w.astype(BF16)
    fg = final_g.reshape(1, D_MODEL)
    xt = x.reshape(TOKENS, D_MODEL)
    row = lambda v: v.reshape(1, -1)
    for l in range(DEPTH):
        xt = _ffn_call(xt, mods, row(norm_g[l, 0]), w1b[l, 0], w3b[l, 0], w2b[l, 0], fg, l, 0, False)
        if l % 2 == 0:
            e = l // 2
            a, q, k, v = _mix_in_call(xt, mods, row(norm_g[l, 1]), tab, w_in_b[e], l)
            o = _attn_call(q, k, v, diff_lambda[e], row(subln_g[e]), l)
            xt = _mix_out_call(xt, mods, a, o.reshape(TOKENS, DIFF_WIDTH), conv_w[e], row(conv_b[e]),
                               row(conv_ln_g[e]), row(conv_ln_b[e]), w_out_b[e], l)
        else:
            o = l // 2
            xt = _pool_call(xt, mods, row(norm_g[l, 1]), pool_w_b[o], row(pool_scale[o]), l)
        xt = _ffn_call(xt, mods, row(norm_g[l, 2]), w1b[l, 1], w3b[l, 1], w2b[l, 1], fg, l, 2,
                       l == DEPTH - 1)
    return xt.reshape(BATCH, SEQ, D_MODEL)
```

```python
import functools

import numpy as np
import jax
import jax.numpy as jnp
from jax import lax
from jax.experimental import pallas as pl
from jax.experimental.pallas import tpu as pltpu

D_MODEL = 1024
BATCH = 8
SEQ = 2048
DEPTH = 4
CHUNK = 64
CONV_CH = D_MODEL // 2
CONV_WIDTH = 31
DIFF_HEADS = 4
DIFF_HEAD_DIM = 64
DIFF_V_DIM = 2 * DIFF_HEAD_DIM
DIFF_WIDTH = DIFF_HEADS * DIFF_V_DIM
IN_COLS = 2 * CONV_CH + 3 * DIFF_WIDTH
MIX_WIDTH = CONV_CH + DIFF_WIDTH
ROPE_THETA = 500000.0
ROT_DIM = DIFF_HEAD_DIM // 4
ROT_HALF = ROT_DIM // 2
POOL_WINDOWS = (2, 4, 8, 16)
POOL_GROUP_CH = D_MODEL // len(POOL_WINDOWS)
D_FF = 2816
N_SUB = 3
EPS = 1e-6

TOKENS = BATCH * SEQ
MOD_ROWS_PER_BATCH = N_SUB * 3
MOD_ROWS_PER_LAYER = BATCH * MOD_ROWS_PER_BATCH

V7X_VMEM_LIMIT_BYTES = 56 * 1024 * 1024
LANES = 128
SUBLANES = 8
HALO = 32
POOL_HALO = 16

TM_FFN = 512
TM_MIX = 512
ADA_TN = 1536
ATT_QBLK = 256
Q_SCALE = DIFF_HEAD_DIM ** -0.5 * float(np.log2(np.e))

F32 = jnp.float32
BF16 = jnp.bfloat16


def _silu(v):
    return v * (1.0 / (1.0 + jnp.exp(-v)))


def _sigmoid(v):
    return 1.0 / (1.0 + jnp.exp(-v))


def _modulate(x, g, shift, scale):
    ms = jnp.mean(x * x, axis=-1, keepdims=True)
    return (x * lax.rsqrt(ms + EPS) * g) * (1.0 + scale) + shift


def _mod_spec(layer, sub, which, tiles_per_batch):
    base = layer * MOD_ROWS_PER_LAYER + sub * 3 + which
    return pl.BlockSpec(
        (None, 1, D_MODEL),
        lambda i: (base + (i // tiles_per_batch) * MOD_ROWS_PER_BATCH, 0, 0))


def _pick_spec(shape, *lead):
    tail = tuple(shape[len(lead):])
    return pl.BlockSpec((None,) * len(lead) + tail, lambda *_: tuple(lead) + (0,) * len(tail),
                        pipeline_mode=pl.Buffered(1))


def _params(sem):
    return pltpu.CompilerParams(dimension_semantics=sem, vmem_limit_bytes=V7X_VMEM_LIMIT_BYTES)


def _ada_kernel(c_ref, w_ref, b_ref, o_ref):
    cond = _silu(c_ref[...]).astype(BF16)
    o_ref[...] = jnp.dot(cond, w_ref[...].astype(BF16), preferred_element_type=F32) + b_ref[...]


def _ada_call(c, w_ada, b_ada):
    n = N_SUB * 3 * D_MODEL
    return pl.pallas_call(
        _ada_kernel,
        grid=(DEPTH, n // ADA_TN),
        in_specs=[
            pl.BlockSpec((BATCH, D_MODEL), lambda l, j: (0, 0)),
            pl.BlockSpec((None, D_MODEL, ADA_TN), lambda l, j: (l, 0, j)),
            pl.BlockSpec((None, 1, ADA_TN), lambda l, j: (l, 0, j)),
        ],
        out_specs=pl.BlockSpec((None, BATCH, ADA_TN), lambda l, j: (l, 0, j)),
        out_shape=jax.ShapeDtypeStruct((DEPTH, BATCH, n), F32),
        compiler_params=_params(("arbitrary", "arbitrary")),
        name="ada_mod",
    )(c, w_ada, b_ada.reshape(DEPTH, 1, n))


def _rot_kernel(pos_ref, freq_ref, o_ref):
    ang = freq_ref[...] * pos_ref[...].astype(F32)
    zeros = jnp.zeros((DIFF_HEAD_DIM - ROT_DIM, SEQ), F32)
    slab = jnp.concatenate([jnp.cos(ang), jnp.sin(ang), zeros], axis=0)
    feat = jnp.concatenate([slab, slab], axis=0)
    for t0 in range(0, SEQ, LANES):
        o_ref[t0:t0 + LANES, :] = feat[:, t0:t0 + LANES].T


def _rot_call(positions):
    inv_freq = ROPE_THETA ** (-jnp.arange(0, ROT_DIM, 2, dtype=F32) / ROT_DIM)
    return pl.pallas_call(
        _rot_kernel,
        grid=(BATCH,),
        in_specs=[
            pl.BlockSpec((None, 1, SEQ), lambda b: (b, 0, 0)),
            pl.BlockSpec((ROT_HALF, 1), lambda b: (0, 0)),
        ],
        out_specs=pl.BlockSpec((SEQ, LANES), lambda b: (b, 0)),
        out_shape=jax.ShapeDtypeStruct((TOKENS, LANES), F32),
        compiler_params=_params(("arbitrary",)),
        name="rotary_table",
    )(positions.reshape(BATCH, 1, SEQ), inv_freq.reshape(ROT_HALF, 1))


def _ffn_kernel(x_ref, shift_ref, scale_ref, gate_ref, g_ref, w1_ref, w3_ref, w2_ref, fg_ref,
                o_ref, *, final):
    x = x_ref[...]
    h = _modulate(x, g_ref[...], shift_ref[...], scale_ref[...]).astype(BF16)
    a = jnp.dot(h, w1_ref[...], preferred_element_type=F32)
    b = jnp.dot(h, w3_ref[...], preferred_element_type=F32)
    u = (_silu(a) * b).astype(BF16)
    y = jnp.dot(u, w2_ref[...], preferred_element_type=F32)
    xo = x + (0.5 * gate_ref[...]) * y
    if final:
        ms = jnp.mean(xo * xo, axis=-1, keepdims=True)
        xo = xo * lax.rsqrt(ms + EPS) * fg_ref[...]
    o_ref[...] = xo


def _ffn_call(x, mods, norm_g, w1, w3, w2, final_g, layer, sub, final):
    tpb = SEQ // TM_FFN
    half = sub // 2
    return pl.pallas_call(
        functools.partial(_ffn_kernel, final=final),
        grid=(TOKENS // TM_FFN,),
        in_specs=[
            pl.BlockSpec((TM_FFN, D_MODEL), lambda i: (i, 0)),
            _mod_spec(layer, sub, 0, tpb),
            _mod_spec(layer, sub, 1, tpb),
            _mod_spec(layer, sub, 2, tpb),
            _pick_spec(norm_g.shape, layer, sub),
            _pick_spec(w1.shape, layer, half),
            _pick_spec(w3.shape, layer, half),
            _pick_spec(w2.shape, layer, half),
            _pick_spec(final_g.shape),
        ],
        out_specs=pl.BlockSpec((TM_FFN, D_MODEL), lambda i: (i, 0)),
        out_shape=jax.ShapeDtypeStruct((TOKENS, D_MODEL), F32),
        compiler_params=_params(("arbitrary",)),
        name=f"ffn_l{layer}_s{sub}",
    )(x, mods, mods, mods, norm_g, w1, w3, w2, final_g)


def _mix_in_kernel(x_ref, shift_ref, scale_ref, g_ref, tab_ref, w_ref,
                   a_ref, q_ref, k_ref, vt_ref):
    h = _modulate(x_ref[...], g_ref[...], shift_ref[...], scale_ref[...]).astype(BF16)
    z = jnp.dot(h, w_ref[...], preferred_element_type=F32)
    a_ref[...] = z[:, :CONV_CH] * _sigmoid(z[:, CONV_CH:2 * CONV_CH])
    tab = tab_ref[...]
    dim = lax.broadcasted_iota(jnp.int32, (1, LANES), 1) % DIFF_HEAD_DIM
    lo, hi = dim < ROT_HALF, (dim >= ROT_HALF) & (dim < ROT_DIM)
    cmul = jnp.where(lo, tab, jnp.where(hi, pltpu.roll(tab, ROT_HALF, 1), 1.0))
    s_up = jnp.where(lo, -pltpu.roll(tab, LANES - ROT_HALF, 1), 0.0)
    s_dn = jnp.where(hi, tab, 0.0)
    q_off = 2 * CONV_CH
    k_off = q_off + DIFF_WIDTH
    for hd in range(DIFF_HEADS):
        for off, ref, mul in ((q_off, q_ref, Q_SCALE), (k_off, k_ref, None)):
            t = z[:, off + hd * LANES: off + (hd + 1) * LANES]
            r = (t * cmul + pltpu.roll(t, LANES - ROT_HALF, 1) * s_up
                 + pltpu.roll(t, ROT_HALF, 1) * s_dn)
            if mul is not None:
                r = r * mul
            ref[:, hd * LANES:(hd + 1) * LANES] = r.astype(BF16)
    vt_ref[...] = z[:, k_off + DIFF_WIDTH:].T.astype(BF16)


def _mix_in_call(x, mods, norm_g, tab, w_in, layer):
    tpb = SEQ // TM_MIX
    tok = lambda w: pl.BlockSpec((TM_MIX, w), lambda i: (i, 0))
    return pl.pallas_call(
        _mix_in_kernel,
        grid=(TOKENS // TM_MIX,),
        in_specs=[
            tok(D_MODEL),
            _mod_spec(layer, 1, 0, tpb),
            _mod_spec(layer, 1, 1, tpb),
            _pick_spec(norm_g.shape, layer, 1),
            tok(LANES),
            _pick_spec(w_in.shape, layer // 2),
        ],
        out_specs=[tok(CONV_CH), tok(DIFF_WIDTH), tok(DIFF_WIDTH),
                   pl.BlockSpec((None, DIFF_WIDTH, TM_MIX), lambda i: (i // tpb, 0, i % tpb))],
        out_shape=[
            jax.ShapeDtypeStruct((TOKENS, CONV_CH), F32),
            jax.ShapeDtypeStruct((TOKENS, DIFF_WIDTH), BF16),
            jax.ShapeDtypeStruct((TOKENS, DIFF_WIDTH), BF16),
            jax.ShapeDtypeStruct((BATCH, DIFF_WIDTH, SEQ), BF16),
        ],
        compiler_params=_params(("arbitrary",)),
        name=f"mix_in_l{layer}",
    )(x, mods, mods, norm_g, tab, w_in)


def _attn_kernel(q_ref, k_ref, vt_ref, dl_ref, sg_ref, o_ref, *, lambda_init):
    dl = dl_ref[...]
    lam = (jnp.exp(jnp.sum(dl[0:1] * dl[1:2], axis=1, keepdims=True))
           - jnp.exp(jnp.sum(dl[2:3] * dl[3:4], axis=1, keepdims=True)) + lambda_init)
    first = lax.broadcasted_iota(jnp.int32, (1, LANES), 1) < DIFF_HEAD_DIM
    out_gain = sg_ref[...] * (1.0 - lambda_init)
    nt = (((1,), (1,)), ((), ()))
    visible = ((lax.broadcasted_iota(jnp.int32, (ATT_QBLK, 1), 0) // CHUNK)
               <= (lax.broadcasted_iota(jnp.int32, (1, 2 * ATT_QBLK), 1) % ATT_QBLK // CHUNK))

    def scores(blk):
        row0 = blk * ATT_QBLK
        klen = row0 + ATT_QBLK
        q = q_ref[row0:klen, :]
        q_both = jnp.concatenate([jnp.where(first, q, jnp.zeros_like(q)),
                                  jnp.where(first, jnp.zeros_like(q), q)], axis=0)
        return lax.dot_general(k_ref[0:klen, :], q_both, nt, preferred_element_type=F32)

    n_blk = SEQ // ATT_QBLK
    s_next = scores(0)
    for blk in range(n_blk):
        row0 = blk * ATT_QBLK
        klen = row0 + ATT_QBLK
        s = s_next
        if blk + 1 < n_blk:
            s_next = scores(blk + 1)
        s_diag = jnp.where(visible, s[row0:, :], -jnp.inf)
        m = jnp.max(s_diag, axis=0, keepdims=True)
        if row0:
            s_full = s[:row0, :]
            m = jnp.maximum(m, jnp.max(s_full, axis=0, keepdims=True))
        p_diag = jnp.exp2(s_diag - m)
        den = jnp.sum(p_diag, axis=0, keepdims=True)
        if row0:
            p_full = jnp.exp2(s_full - m)
            den = den + jnp.sum(p_full, axis=0, keepdims=True)
        den0, den1 = den[:, :ATT_QBLK], den[:, ATT_QBLK:]
        ratio = lam * den0 / den1
        w = (p_diag[:, :ATT_QBLK] - ratio * p_diag[:, ATT_QBLK:]).astype(BF16)
        if row0:
            w = jnp.concatenate(
                [(p_full[:, :ATT_QBLK] - ratio * p_full[:, ATT_QBLK:]).astype(BF16), w], axis=0)
        o = jnp.dot(vt_ref[:, 0:klen], w, preferred_element_type=F32) * (1.0 / den0)
        ms = jnp.mean(o * o, axis=0, keepdims=True)
        o_ref[row0:klen, :] = ((o * lax.rsqrt(ms + EPS)).T * out_gain).astype(BF16)


def _attn_call(q, k, vt, diff_lambda, subln_g, layer):
    lambda_init = 0.8 - 0.6 * float(np.exp(-0.3 * layer))
    head = lambda: pl.BlockSpec((None, SEQ, LANES), lambda b, h: (b, 0, h))
    shp = (BATCH, SEQ, DIFF_WIDTH)
    return pl.pallas_call(
        functools.partial(_attn_kernel, lambda_init=lambda_init),
        grid=(BATCH, DIFF_HEADS),
        in_specs=[
            head(), head(),
            pl.BlockSpec((None, DIFF_V_DIM, SEQ), lambda b, h: (b, h, 0)),
            _pick_spec(diff_lambda.shape, layer // 2),
            _pick_spec(subln_g.shape, layer // 2),
        ],
        out_specs=head(),
        out_shape=jax.ShapeDtypeStruct(shp, BF16),
        compiler_params=_params(("arbitrary", "arbitrary")),
        name=f"diff_attn_l{layer}",
    )(q.reshape(shp), k.reshape(shp), vt, diff_lambda, subln_g)


def _mix_out_kernel(x_ref, gate_ref, a_ref, halo_ref, o_ref, cw_ref, cb_ref, lg_ref, lb_ref, w_ref,
                    out_ref, abuf, conv):
    i = pl.program_id(0)
    seq_start = (i % (SEQ // TM_MIX)) == 0
    halo = halo_ref[...]
    abuf[0:HALO, :] = jnp.where(seq_start, jnp.zeros_like(halo), halo)
    abuf[HALO:, :] = a_ref[...]
    rb = 32
    lead = HALO - (CONV_WIDTH - 1)
    for blk in range(TM_MIX // rb):
        acc = jnp.broadcast_to(cb_ref[...], (rb, CONV_CH))
        for j in range(CONV_WIDTH):
            start = blk * rb + lead + j
            acc = acc + cw_ref[j:j + 1, :] * abuf[start:start + rb, :]
        conv[blk * rb:(blk + 1) * rb, :] = acc
    c = conv[...]
    mu = jnp.mean(c, axis=-1, keepdims=True)
    d = c - mu
    var = jnp.mean(d * d, axis=-1, keepdims=True)
    a = _silu(d * lax.rsqrt(var + EPS) * lg_ref[...] + lb_ref[...]).astype(BF16)
    y = (jnp.dot(a, w_ref[0:CONV_CH, :], preferred_element_type=F32)
         + jnp.dot(o_ref[...], w_ref[CONV_CH:, :], preferred_element_type=F32))
    out_ref[...] = x_ref[...] + gate_ref[...] * y


def _mix_out_call(x, mods, a, o, conv_w, conv_b, ln_g, ln_b, w_out, layer):
    tpb = SEQ // TM_MIX
    e = layer // 2
    tok = lambda w: pl.BlockSpec((TM_MIX, w), lambda i: (i, 0))
    halo_blocks = TM_MIX // HALO
    return pl.pallas_call(
        _mix_out_kernel,
        grid=(TOKENS // TM_MIX,),
        in_specs=[
            tok(D_MODEL),
            _mod_spec(layer, 1, 2, tpb),
            tok(CONV_CH),
            pl.BlockSpec((HALO, CONV_CH), lambda i: (jnp.maximum(i * halo_blocks - 1, 0), 0)),
            tok(DIFF_WIDTH),
            _pick_spec(conv_w.shape, e),
            _pick_spec(conv_b.shape, e),
            _pick_spec(ln_g.shape, e),
            _pick_spec(ln_b.shape, e),
            _pick_spec(w_out.shape, e),
        ],
        out_specs=tok(D_MODEL),
        out_shape=jax.ShapeDtypeStruct((TOKENS, D_MODEL), F32),
        scratch_shapes=[pltpu.VMEM((HALO + TM_MIX, CONV_CH), F32),
                        pltpu.VMEM((TM_MIX, CONV_CH), F32)],
        compiler_params=_params(("arbitrary",)),
        name=f"mix_out_l{layer}",
    )(x, mods, a, a, o, conv_w, conv_b, ln_g, ln_b, w_out)


def _pool_kernel(x_ref, xh_ref, shift_ref, scale_ref, gate_ref, g_ref, pw_ref, ps_ref,
                 out_ref, hbuf):
    i = pl.program_id(0)
    t_in_seq = (i % (SEQ // TM_MIX)) * TM_MIX
    x = x_ref[...]
    g, shift, scale = g_ref[...], shift_ref[...], scale_ref[...]
    hh = _modulate(xh_ref[...], g, shift, scale)
    hbuf[0:POOL_HALO, :] = jnp.where(t_in_seq == 0, jnp.zeros_like(hh), hh)
    hbuf[POOL_HALO:, :] = _modulate(x, g, shift, scale)
    pos = (t_in_seq + 1 + lax.broadcasted_iota(jnp.int32, (TM_MIX, 1), 0)).astype(F32)
    ys = []
    for grp, win in enumerate(POOL_WINDOWS):
        lo, hi = grp * POOL_GROUP_CH, (grp + 1) * POOL_GROUP_CH
        cur = hbuf[POOL_HALO:, lo:hi]
        tot = cur
        for lag in range(1, win):
            tot = tot + hbuf[POOL_HALO - lag:POOL_HALO - lag + TM_MIX, lo:hi]
        p = (tot / jnp.minimum(pos, float(win)) - cur).astype(BF16)
        ys.append(jnp.dot(p, pw_ref[grp], preferred_element_type=F32))
    y = jnp.concatenate(ys, axis=-1) * ps_ref[...]
    out_ref[...] = x + gate_ref[...] * y


def _pool_call(x, mods, norm_g, pool_w, pool_scale, layer):
    tpb = SEQ // TM_MIX
    tok = pl.BlockSpec((TM_MIX, D_MODEL), lambda i: (i, 0))
    halo_blocks = TM_MIX // POOL_HALO
    return pl.pallas_call(
        _pool_kernel,
        grid=(TOKENS // TM_MIX,),
        in_specs=[
            tok,
            pl.BlockSpec((POOL_HALO, D_MODEL), lambda i: (jnp.maximum(i * halo_blocks - 1, 0), 0)),
            _mod_spec(layer, 1, 0, tpb),
            _mod_spec(layer, 1, 1, tpb),
            _mod_spec(layer, 1, 2, tpb),
            _pick_spec(norm_g.shape, layer, 1),
            _pick_spec(pool_w.shape, layer // 2),
            _pick_spec(pool_scale.shape, layer // 2),
        ],
        out_specs=tok,
        out_shape=jax.ShapeDtypeStruct((TOKENS, D_MODEL), F32),
        scratch_shapes=[pltpu.VMEM((POOL_HALO + TM_MIX, D_MODEL), F32)],
        compiler_params=_params(("arbitrary",)),
        name=f"pool_l{layer}",
    )(x, x, mods, mods, mods, norm_g, pool_w, pool_scale)


def kernel(x, c, positions, w_ada, b_ada, norm_g, ffn_w1, ffn_w3, ffn_w2, w_in, w_out, conv_w, conv_b,
           conv_ln_g, conv_ln_b, diff_lambda, subln_g, pool_w, pool_scale, final_g):
    mods = _ada_call(c, w_ada, b_ada).reshape(DEPTH * MOD_ROWS_PER_LAYER, 1, D_MODEL)
    tab = _rot_call(positions)
    w1, w3, w2 = ffn_w1.astype(BF16), ffn_w3.astype(BF16), ffn_w2.astype(BF16)
    w_in, w_out, pool_w = w_in.astype(BF16), w_out.astype(BF16), pool_w.astype(BF16)
    rows = lambda v: v.reshape(v.shape[:-1] + (1, v.shape[-1]))
    norm_g, final_g = rows(norm_g), rows(final_g)
    conv_b, conv_ln_g, conv_ln_b = rows(conv_b), rows(conv_ln_g), rows(conv_ln_b)
    subln_g, pool_scale = rows(subln_g), rows(pool_scale)
    xt = x.reshape(TOKENS, D_MODEL)
    for l in range(DEPTH):
        xt = _ffn_call(xt, mods, norm_g, w1, w3, w2, final_g, l, 0, False)
        if l % 2 == 0:
            a, q, k, v = _mix_in_call(xt, mods, norm_g, tab, w_in, l)
            o = _attn_call(q, k, v, diff_lambda, subln_g, l)
            xt = _mix_out_call(xt, mods, a, o.reshape(TOKENS, DIFF_WIDTH), conv_w, conv_b,
                               conv_ln_g, conv_ln_b, w_out, l)
        else:
            xt = _pool_call(xt, mods, norm_g, pool_w, pool_scale, l)
        xt = _ffn_call(xt, mods, norm_g, w1, w3, w2, final_g, l, 2, l == DEPTH - 1)
    return xt.reshape(BATCH, SEQ, D_MODEL)
```

```python
import functools

import numpy as np
import jax
import jax.numpy as jnp
from jax import lax
from jax.experimental import pallas as pl
from jax.experimental.pallas import tpu as pltpu

D_MODEL = 1024
BATCH = 8
SEQ = 2048
DEPTH = 4
CHUNK = 64
CONV_CH = D_MODEL // 2
CONV_WIDTH = 31
DIFF_HEADS = 4
DIFF_HEAD_DIM = 64
DIFF_V_DIM = 2 * DIFF_HEAD_DIM
DIFF_WIDTH = DIFF_HEADS * DIFF_V_DIM
IN_COLS = 2 * CONV_CH + 3 * DIFF_WIDTH
MIX_WIDTH = CONV_CH + DIFF_WIDTH
ROPE_THETA = 500000.0
ROT_DIM = DIFF_HEAD_DIM // 4
ROT_HALF = ROT_DIM // 2
POOL_WINDOWS = (2, 4, 8, 16)
POOL_GROUP_CH = D_MODEL // len(POOL_WINDOWS)
D_FF = 2816
N_SUB = 3
EPS = 1e-6

TOKENS = BATCH * SEQ
MOD_ROWS_PER_BATCH = N_SUB * 3
MOD_ROWS_PER_LAYER = BATCH * MOD_ROWS_PER_BATCH

V7X_VMEM_LIMIT_BYTES = 56 * 1024 * 1024
LANES = 128
SUBLANES = 8
HALO = 32
POOL_HALO = 32
CONV_WIN = 512
CONV_ROWS = 32

TM = 512
TM_FFN = 1024
FFN_SUB = 256
N_TILES = TOKENS // TM
TILES_PER_SEQ = SEQ // TM
ADA_TN = 1536
ATT_QBLK = 256
ATT_ONES_ROWS = 16
V_EXT = DIFF_V_DIM + ATT_ONES_ROWS
ATT_AHEAD = 2
Q_SCALE = DIFF_HEAD_DIM ** -0.5 * float(np.log2(np.e))

F32 = jnp.float32
BF16 = jnp.bfloat16


def _silu(v):
    return v * (1.0 / (1.0 + jnp.exp(-v)))


def _sigmoid(v):
    return 1.0 / (1.0 + jnp.exp(-v))


def _modulate(x, g, shift, scale):
    ms = jnp.mean(x * x, axis=-1, keepdims=True)
    return (x * lax.rsqrt(ms + EPS) * g) * (1.0 + scale) + shift


def _mod_spec(layer, sub, which, tile=TM):
    base = layer * MOD_ROWS_PER_LAYER + sub * 3 + which
    return pl.BlockSpec(
        (None, 1, D_MODEL),
        lambda i: (base + (i // (SEQ // tile)) * MOD_ROWS_PER_BATCH, 0, 0))


def _pick_spec(shape, *lead):
    tail = tuple(shape[len(lead):])
    return pl.BlockSpec((None,) * len(lead) + tail, lambda *_: tuple(lead) + (0,) * len(tail),
                        pipeline_mode=pl.Buffered(1))


def _params(sem):
    return pltpu.CompilerParams(dimension_semantics=sem, vmem_limit_bytes=V7X_VMEM_LIMIT_BYTES)


def _ada_kernel(c_ref, w_ref, b_ref, o_ref):
    cond = _silu(c_ref[...]).astype(BF16)
    o_ref[...] = jnp.dot(cond, w_ref[...].astype(BF16), preferred_element_type=F32) + b_ref[...]


def _ada_call(c, w_ada, b_ada):
    n = N_SUB * 3 * D_MODEL
    return pl.pallas_call(
        _ada_kernel,
        grid=(DEPTH, n // ADA_TN),
        in_specs=[
            pl.BlockSpec((BATCH, D_MODEL), lambda l, j: (0, 0)),
            pl.BlockSpec((None, D_MODEL, ADA_TN), lambda l, j: (l, 0, j)),
            pl.BlockSpec((None, 1, ADA_TN), lambda l, j: (l, 0, j)),
        ],
        out_specs=pl.BlockSpec((None, BATCH, ADA_TN), lambda l, j: (l, 0, j)),
        out_shape=jax.ShapeDtypeStruct((DEPTH, BATCH, n), F32),
        compiler_params=_params(("arbitrary", "arbitrary")),
        name="ada_mod",
    )(c, w_ada, b_ada.reshape(DEPTH, 1, n))


def _rot_kernel(pos_ref, freq_ref, o_ref):
    ang = freq_ref[...] * pos_ref[...].astype(F32)
    zeros = jnp.zeros((DIFF_HEAD_DIM - ROT_DIM, SEQ), F32)
    slab = jnp.concatenate([jnp.cos(ang), jnp.sin(ang), zeros], axis=0)
    feat = jnp.concatenate([slab, slab], axis=0)
    for t0 in range(0, SEQ, LANES):
        o_ref[t0:t0 + LANES, :] = feat[:, t0:t0 + LANES].T


def _rot_call(positions):
    inv_freq = ROPE_THETA ** (-jnp.arange(0, ROT_DIM, 2, dtype=F32) / ROT_DIM)
    return pl.pallas_call(
        _rot_kernel,
        grid=(BATCH,),
        in_specs=[
            pl.BlockSpec((None, 1, SEQ), lambda b: (b, 0, 0)),
            pl.BlockSpec((ROT_HALF, 1), lambda b: (0, 0)),
        ],
        out_specs=pl.BlockSpec((SEQ, LANES), lambda b: (b, 0)),
        out_shape=jax.ShapeDtypeStruct((TOKENS, LANES), F32),
        compiler_params=_params(("arbitrary",)),
        name="rotary_table",
    )(positions.reshape(BATCH, 1, SEQ), inv_freq.reshape(ROT_HALF, 1))


def _mix_in_kernel(x_ref, shift_ref, scale_ref, g_ref, tab_ref, w_ref,
                   a_ref, q_ref, k_ref, vt_ref):
    h = _modulate(x_ref[...], g_ref[...], shift_ref[...], scale_ref[...]).astype(BF16)
    z = jnp.dot(h, w_ref[...], preferred_element_type=F32)
    a_ref[...] = z[:, :CONV_CH] * _sigmoid(z[:, CONV_CH:2 * CONV_CH])
    tab = tab_ref[...]
    dim = lax.broadcasted_iota(jnp.int32, (1, LANES), 1) % DIFF_HEAD_DIM
    lo, hi = dim < ROT_HALF, (dim >= ROT_HALF) & (dim < ROT_DIM)
    cmul = jnp.where(lo, tab, jnp.where(hi, pltpu.roll(tab, ROT_HALF, 1), 1.0))
    s_up = jnp.where(lo, -pltpu.roll(tab, LANES - ROT_HALF, 1), 0.0)
    s_dn = jnp.where(hi, tab, 0.0)
    q_off = 2 * CONV_CH
    k_off = q_off + DIFF_WIDTH
    for hd in range(DIFF_HEADS):
        for off, ref, mul in ((q_off, q_ref, Q_SCALE), (k_off, k_ref, None)):
            t = z[:, off + hd * LANES: off + (hd + 1) * LANES]
            r = (t * cmul + pltpu.roll(t, LANES - ROT_HALF, 1) * s_up
                 + pltpu.roll(t, ROT_HALF, 1) * s_dn)
            if mul is not None:
                r = r * mul
            ref[:, hd * LANES:(hd + 1) * LANES] = r.astype(BF16)
    v_off = k_off + DIFF_WIDTH
    for hd in range(DIFF_HEADS):
        r0 = hd * V_EXT
        vt_ref[r0:r0 + DIFF_V_DIM, :] = z[:, v_off + hd * LANES:v_off + (hd + 1) * LANES].T.astype(BF16)
        vt_ref[r0 + DIFF_V_DIM:r0 + V_EXT, :] = jnp.ones((ATT_ONES_ROWS, TM), BF16)


def _mix_in_call(x, mods, norm_g, tab, w_in, layer):
    tok = lambda w: pl.BlockSpec((TM, w), lambda i: (i, 0))
    return pl.pallas_call(
        _mix_in_kernel,
        grid=(N_TILES,),
        in_specs=[
            tok(D_MODEL),
            _mod_spec(layer, 1, 0),
            _mod_spec(layer, 1, 1),
            _pick_spec(norm_g.shape, layer, 1),
            tok(LANES),
            _pick_spec(w_in.shape, layer // 2),
        ],
        out_specs=[tok(CONV_CH), tok(DIFF_WIDTH), tok(DIFF_WIDTH),
                   pl.BlockSpec((None, DIFF_HEADS * V_EXT, TM),
                                lambda i: (i // TILES_PER_SEQ, 0, i % TILES_PER_SEQ))],
        out_shape=[
            jax.ShapeDtypeStruct((TOKENS, CONV_CH), F32),
            jax.ShapeDtypeStruct((TOKENS, DIFF_WIDTH), BF16),
            jax.ShapeDtypeStruct((TOKENS, DIFF_WIDTH), BF16),
            jax.ShapeDtypeStruct((BATCH, DIFF_HEADS * V_EXT, SEQ), BF16),
        ],
        compiler_params=_params(("arbitrary",)),
        name=f"mix_in_l{layer}",
    )(x, mods, mods, norm_g, tab, w_in)


def _attn_kernel(q_ref, k_ref, vt_ext, dl_ref, sg_ref, o_ref, *, lambda_init):
    dl = dl_ref[...]
    lam = (jnp.exp(jnp.sum(dl[0:1] * dl[1:2], axis=1, keepdims=True))
           - jnp.exp(jnp.sum(dl[2:3] * dl[3:4], axis=1, keepdims=True)) + lambda_init)
    first = lax.broadcasted_iota(jnp.int32, (1, LANES), 1) < DIFF_HEAD_DIM
    out_gain = sg_ref[...] * (1.0 - lambda_init)
    nt = (((1,), (1,)), ((), ()))
    visible = ((lax.broadcasted_iota(jnp.int32, (ATT_QBLK, 1), 0) // CHUNK)
               <= (lax.broadcasted_iota(jnp.int32, (1, 2 * ATT_QBLK), 1) % ATT_QBLK // CHUNK))

    def scores(blk):
        row0 = blk * ATT_QBLK
        klen = row0 + ATT_QBLK
        q = q_ref[row0:klen, :]
        q_both = jnp.concatenate([jnp.where(first, q, jnp.zeros_like(q)),
                                  jnp.where(first, jnp.zeros_like(q), q)], axis=0)
        return lax.dot_general(k_ref[0:klen, :], q_both, nt, preferred_element_type=F32)

    n_blk = SEQ // ATT_QBLK
    ahead = [scores(blk) for blk in range(ATT_AHEAD)]
    for blk in range(n_blk):
        row0 = blk * ATT_QBLK
        klen = row0 + ATT_QBLK
        s = ahead.pop(0)
        if blk + ATT_AHEAD < n_blk:
            ahead.append(scores(blk + ATT_AHEAD))
        s_diag = jnp.where(visible, s[row0:, :], -jnp.inf)
        m = jnp.max(s_diag, axis=0, keepdims=True)
        if row0:
            s_full = s[:row0, :]
            m = jnp.maximum(m, jnp.max(s_full, axis=0, keepdims=True))
        p = jnp.exp2(s_diag - m).astype(BF16)
        if row0:
            p = jnp.concatenate([jnp.exp2(s_full - m).astype(BF16), p], axis=0)
        pv = jnp.dot(vt_ext[:, 0:klen], p, preferred_element_type=F32)
        inv = 1.0 / pv[DIFF_V_DIM:DIFF_V_DIM + 1, :]
        o = (pv[:DIFF_V_DIM, :ATT_QBLK] * inv[:, :ATT_QBLK]
             - pv[:DIFF_V_DIM, ATT_QBLK:] * (lam * inv[:, ATT_QBLK:]))
        ms = jnp.mean(o * o, axis=0, keepdims=True)
        o_ref[row0:klen, :] = ((o * lax.rsqrt(ms + EPS)).T * out_gain).astype(BF16)


def _attn_call(q, k, vt, diff_lambda, subln_g, layer):
    lambda_init = 0.8 - 0.6 * float(np.exp(-0.3 * layer))
    head = lambda: pl.BlockSpec((None, SEQ, LANES), lambda b, h: (b, 0, h))
    shp = (BATCH, SEQ, DIFF_WIDTH)
    return pl.pallas_call(
        functools.partial(_attn_kernel, lambda_init=lambda_init),
        grid=(BATCH, DIFF_HEADS),
        in_specs=[
            head(), head(),
            pl.BlockSpec((None, V_EXT, SEQ), lambda b, h: (b, h, 0)),
            _pick_spec(diff_lambda.shape, layer // 2),
            _pick_spec(subln_g.shape, layer // 2),
        ],
        out_specs=head(),
        out_shape=jax.ShapeDtypeStruct(shp, BF16),
        compiler_params=_params(("arbitrary", "arbitrary")),
        name=f"diff_attn_l{layer}",
    )(q.reshape(shp), k.reshape(shp), vt, diff_lambda, subln_g)


def _conv_out_stage(x_ref, gate_ref, a_ref, halo_ref, o_ref, cw_ref, cb_ref, lg_ref, lb_ref, w_ref,
                    abuf, shifted, conv, wrep):
    seq_start = (pl.program_id(0) % TILES_PER_SEQ) == 0
    halo = halo_ref[...]
    abuf[0:HALO, :] = jnp.where(seq_start, jnp.zeros_like(halo), halo)
    abuf[HALO:, :] = a_ref[...]
    lead = HALO - (CONV_WIDTH - 1)
    span = CONV_WIN + HALO - SUBLANES
    for j in range(CONV_WIDTH):
        wrep[j] = jnp.broadcast_to(cw_ref[j:j + 1, :], (SUBLANES, CONV_CH))
    bias = jnp.broadcast_to(cb_ref[...], (SUBLANES, CONV_CH))
    groups = CONV_ROWS // SUBLANES
    for base in range(0, TM, CONV_WIN):
        for r in range(1, SUBLANES):
            shifted[r - 1, 0:span, :] = abuf[base + r:base + r + span, :]
        for r0 in range(0, CONV_WIN, CONV_ROWS):
            accs = [bias] * groups
            for j in range(CONV_WIDTH):
                whole, r = divmod(lead + j, SUBLANES)
                w = wrep[j]
                for grp in range(groups):
                    start = r0 + (whole + grp) * SUBLANES
                    if r == 0:
                        src = abuf[base + start:base + start + SUBLANES, :]
                    else:
                        src = shifted[r - 1, start:start + SUBLANES, :]
                    accs[grp] = accs[grp] + w * src
            for grp in range(groups):
                row = base + r0 + grp * SUBLANES
                conv[row:row + SUBLANES, :] = accs[grp]
    c = conv[...]
    mu = jnp.mean(c, axis=-1, keepdims=True)
    d = c - mu
    var = jnp.mean(d * d, axis=-1, keepdims=True)
    a = _silu(d * lax.rsqrt(var + EPS) * lg_ref[...] + lb_ref[...]).astype(BF16)
    y = (jnp.dot(a, w_ref[0:CONV_CH, :], preferred_element_type=F32)
         + jnp.dot(o_ref[...], w_ref[CONV_CH:, :], preferred_element_type=F32))
    return x_ref[...] + gate_ref[...] * y


def _pool_stage(x_ref, xh_ref, shift_ref, scale_ref, gate_ref, g_ref, pw_ref, ps_ref, hbuf, level):
    t_in_seq = (pl.program_id(0) % TILES_PER_SEQ) * TM
    x = x_ref[...]
    g, shift, scale = g_ref[...], shift_ref[...], scale_ref[...]
    hh = _modulate(xh_ref[...], g, shift, scale)
    hbuf[0:POOL_HALO, :] = jnp.where(t_in_seq == 0, jnp.zeros_like(hh), hh)
    hbuf[POOL_HALO:, :] = _modulate(x, g, shift, scale)
    pos = (t_in_seq + 1 + lax.broadcasted_iota(jnp.int32, (TM, 1), 0)).astype(F32)
    rows = POOL_HALO + TM
    ys = []
    for grp, win in enumerate(POOL_WINDOWS):
        lo, hi = grp * POOL_GROUP_CH, (grp + 1) * POOL_GROUP_CH
        cur = hbuf[POOL_HALO:, lo:hi]
        src = lambda r0, r1: hbuf[r0:r1, lo:hi]
        lag, k = 1, 0
        while 2 * lag < win:
            start = SUBLANES * (k + 1)
            level[k, start:rows, :] = src(start, rows) + src(start - lag, rows - lag)
            src = lambda r0, r1, k=k: level[k, r0:r1, :]
            lag, k = 2 * lag, k + 1
        tot = src(POOL_HALO, rows) + src(POOL_HALO - lag, rows - lag)
        p = (tot / jnp.minimum(pos, float(win)) - cur).astype(BF16)
        ys.append(jnp.dot(p, pw_ref[grp], preferred_element_type=F32))
    y = jnp.concatenate(ys, axis=-1) * ps_ref[...]
    return x + gate_ref[...] * y


def _mixer_kernel(*refs, stage, n_in):
    out_ref = refs[n_in]
    out_ref[...] = stage(*refs[:n_in], *refs[n_in + 1:])


def _tok_spec(width):
    return pl.BlockSpec((TM, width), lambda i: (i, 0))


def _halo_spec(rows, width):
    return pl.BlockSpec((rows, width), lambda i: (jnp.maximum(i * (TM // rows) - 1, 0), 0))


def _mixer_call(stage, args, specs, scratch, name):
    return pl.pallas_call(
        functools.partial(_mixer_kernel, stage=stage, n_in=len(args)),
        grid=(N_TILES,),
        in_specs=specs,
        out_specs=_tok_spec(D_MODEL),
        out_shape=jax.ShapeDtypeStruct((TOKENS, D_MODEL), F32),
        scratch_shapes=scratch,
        compiler_params=_params(("arbitrary",)),
        name=name,
    )(*args)


def _conv_out_call(x, mods, a, o, conv_w, conv_b, ln_g, ln_b, w_out, layer):
    e = layer // 2
    specs = [
        _tok_spec(D_MODEL),
        _mod_spec(layer, 1, 2),
        _tok_spec(CONV_CH),
        _halo_spec(HALO, CONV_CH),
        _tok_spec(DIFF_WIDTH),
        _pick_spec(conv_w.shape, e),
        _pick_spec(conv_b.shape, e),
        _pick_spec(ln_g.shape, e),
        _pick_spec(ln_b.shape, e),
        _pick_spec(w_out.shape, e),
    ]
    scratch = [pltpu.VMEM((HALO + TM, CONV_CH), F32),
               pltpu.VMEM((SUBLANES - 1, CONV_WIN + HALO - SUBLANES, CONV_CH), F32),
               pltpu.VMEM((TM, CONV_CH), F32),
               pltpu.VMEM((CONV_WIDTH, SUBLANES, CONV_CH), F32)]
    return _mixer_call(_conv_out_stage, [x, mods, a, a, o, conv_w, conv_b, ln_g, ln_b, w_out],
                       specs, scratch, f"mix_out_l{layer}")


def _pool_call(x, mods, norm_g, pool_w, pool_scale, layer):
    specs = [
        _tok_spec(D_MODEL),
        _halo_spec(POOL_HALO, D_MODEL),
        _mod_spec(layer, 1, 0),
        _mod_spec(layer, 1, 1),
        _mod_spec(layer, 1, 2),
        _pick_spec(norm_g.shape, layer, 1),
        _pick_spec(pool_w.shape, layer // 2),
        _pick_spec(pool_scale.shape, layer // 2),
    ]
    n_levels = int(np.log2(max(POOL_WINDOWS))) - 1
    assert SUBLANES * (n_levels + 1) <= POOL_HALO and max(POOL_WINDOWS) <= POOL_HALO
    scratch = [pltpu.VMEM((POOL_HALO + TM, D_MODEL), F32),
               pltpu.VMEM((n_levels, POOL_HALO + TM, POOL_GROUP_CH), F32)]
    return _mixer_call(_pool_stage, [x, x, mods, mods, mods, norm_g, pool_w, pool_scale],
                       specs, scratch, f"pool_l{layer}")


def _ffn_kernel(x_ref, shift_ref, scale_ref, gate_ref, g_ref, w1_ref, w3_ref, w2_ref, fg_ref,
                o_ref, *, final):
    for r0 in range(0, TM_FFN, FFN_SUB):
        x = x_ref[r0:r0 + FFN_SUB, :]
        h = _modulate(x, g_ref[...], shift_ref[...], scale_ref[...]).astype(BF16)
        a = jnp.dot(h, w1_ref[...], preferred_element_type=F32)
        b = jnp.dot(h, w3_ref[...], preferred_element_type=F32)
        u = (_silu(a) * b).astype(BF16)
        y = jnp.dot(u, w2_ref[...], preferred_element_type=F32)
        xo = x + (0.5 * gate_ref[...]) * y
        if final:
            ms = jnp.mean(xo * xo, axis=-1, keepdims=True)
            xo = xo * lax.rsqrt(ms + EPS) * fg_ref[...]
        o_ref[r0:r0 + FFN_SUB, :] = xo


def _ffn_call(x, mods, norm_g, w1, w3, w2, final_g, layer, sub, final):
    half = sub // 2
    return pl.pallas_call(
        functools.partial(_ffn_kernel, final=final),
        grid=(TOKENS // TM_FFN,),
        in_specs=[
            pl.BlockSpec((TM_FFN, D_MODEL), lambda i: (i, 0)),
            _mod_spec(layer, sub, 0, TM_FFN),
            _mod_spec(layer, sub, 1, TM_FFN),
            _mod_spec(layer, sub, 2, TM_FFN),
            _pick_spec(norm_g.shape, layer, sub),
            _pick_spec(w1.shape, layer, half),
            _pick_spec(w3.shape, layer, half),
            _pick_spec(w2.shape, layer, half),
            _pick_spec(final_g.shape),
        ],
        out_specs=pl.BlockSpec((TM_FFN, D_MODEL), lambda i: (i, 0)),
        out_shape=jax.ShapeDtypeStruct((TOKENS, D_MODEL), F32),
        compiler_params=_params(("arbitrary",)),
        name=f"ffn_l{layer}_s{sub}",
    )(x, mods, mods, mods, norm_g, w1, w3, w2, final_g)


def kernel(x, c, positions, w_ada, b_ada, norm_g, ffn_w1, ffn_w3, ffn_w2, w_in, w_out, conv_w, conv_b,
           conv_ln_g, conv_ln_b, diff_lambda, subln_g, pool_w, pool_scale, final_g):
    mods = _ada_call(c, w_ada, b_ada).reshape(DEPTH * MOD_ROWS_PER_LAYER, 1, D_MODEL)
    tab = _rot_call(positions)
    w1, w3, w2 = ffn_w1.astype(BF16), ffn_w3.astype(BF16), ffn_w2.astype(BF16)
    w_in, w_out, pool_w = w_in.astype(BF16), w_out.astype(BF16), pool_w.astype(BF16)
    rows = lambda v: v.reshape(v.shape[:-1] + (1, v.shape[-1]))
    norm_g, final_g = rows(norm_g), rows(final_g)
    conv_b, conv_ln_g, conv_ln_b = rows(conv_b), rows(conv_ln_g), rows(conv_ln_b)
    subln_g, pool_scale = rows(subln_g), rows(pool_scale)
    ffn = (mods, norm_g, w1, w3, w2, final_g)
    xt = x.reshape(TOKENS, D_MODEL)
    for l in range(DEPTH):
        xt = _ffn_call(xt, *ffn, l, 0, False)
        if l % 2 == 0:
            a, q, k, vt = _mix_in_call(xt, mods, norm_g, tab, w_in, l)
            o = _attn_call(q, k, vt, diff_lambda, subln_g, l)
            xt = _conv_out_call(xt, mods, a, o.reshape(TOKENS, DIFF_WIDTH), conv_w, conv_b,
                                conv_ln_g, conv_ln_b, w_out, l)
        else:
            xt = _pool_call(xt, mods, norm_g, pool_w, pool_scale, l)
        xt = _ffn_call(xt, *ffn, l, 2, l == DEPTH - 1)
    return xt.reshape(BATCH, SEQ, D_MODEL)
```

```python
import functools

import numpy as np
import jax
import jax.numpy as jnp
from jax import lax
from jax.experimental import pallas as pl
from jax.experimental.pallas import tpu as pltpu

D_MODEL = 1024
BATCH = 8
SEQ = 2048
DEPTH = 4
CHUNK = 64
CONV_CH = D_MODEL // 2
CONV_WIDTH = 31
DIFF_HEADS = 4
DIFF_HEAD_DIM = 64
DIFF_V_DIM = 2 * DIFF_HEAD_DIM
DIFF_WIDTH = DIFF_HEADS * DIFF_V_DIM
IN_COLS = 2 * CONV_CH + 3 * DIFF_WIDTH
MIX_WIDTH = CONV_CH + DIFF_WIDTH
ROPE_THETA = 500000.0
ROT_DIM = DIFF_HEAD_DIM // 4
ROT_HALF = ROT_DIM // 2
POOL_WINDOWS = (2, 4, 8, 16)
POOL_GROUP_CH = D_MODEL // len(POOL_WINDOWS)
D_FF = 2816
N_SUB = 3
EPS = 1e-6

TOKENS = BATCH * SEQ
MOD_ROWS_PER_BATCH = N_SUB * 3
MOD_ROWS_PER_LAYER = BATCH * MOD_ROWS_PER_BATCH

V7X_VMEM_LIMIT_BYTES = 56 * 1024 * 1024
LANES = 128
SUBLANES = 8
HALO = 32
POOL_HALO = 32
CONV_WIN = 512
CONV_ROWS = 32

TM = 1024
TM_FFN = 1024
FFN_SUB = 128
FFN_UP_CHUNK = 128
FFN_DOWN_CHUNK = 352
N_TILES = TOKENS // TM
TILES_PER_SEQ = SEQ // TM
ADA_TN = 1536
ATT_QBLK = 256
ATT_ONES_ROWS = 16
V_EXT = DIFF_V_DIM + ATT_ONES_ROWS
ATT_AHEAD = 2
Q_SCALE = DIFF_HEAD_DIM ** -0.5 * float(np.log2(np.e))

F32 = jnp.float32
BF16 = jnp.bfloat16


def _silu(v):
    return v * (1.0 / (1.0 + jnp.exp(-v)))


def _sigmoid(v):
    return 1.0 / (1.0 + jnp.exp(-v))


def _modulate(x, g, shift, scale):
    ms = jnp.mean(x * x, axis=-1, keepdims=True)
    return (x * lax.rsqrt(ms + EPS) * g) * (1.0 + scale) + shift


def _mod_spec(layer, sub, which, tile=TM):
    base = layer * MOD_ROWS_PER_LAYER + sub * 3 + which
    return pl.BlockSpec(
        (None, 1, D_MODEL),
        lambda i: (base + (i // (SEQ // tile)) * MOD_ROWS_PER_BATCH, 0, 0))


def _pick_spec(shape, *lead):
    tail = tuple(shape[len(lead):])
    return pl.BlockSpec((None,) * len(lead) + tail, lambda *_: tuple(lead) + (0,) * len(tail),
                        pipeline_mode=pl.Buffered(1))


def _params(sem):
    return pltpu.CompilerParams(dimension_semantics=sem, vmem_limit_bytes=V7X_VMEM_LIMIT_BYTES)


def _ada_kernel(c_ref, w_ref, b_ref, o_ref):
    cond = _silu(c_ref[...]).astype(BF16)
    o_ref[...] = jnp.dot(cond, w_ref[...].astype(BF16), preferred_element_type=F32) + b_ref[...]


def _ada_call(c, w_ada, b_ada):
    n = N_SUB * 3 * D_MODEL
    return pl.pallas_call(
        _ada_kernel,
        grid=(DEPTH, n // ADA_TN),
        in_specs=[
            pl.BlockSpec((BATCH, D_MODEL), lambda l, j: (0, 0)),
            pl.BlockSpec((None, D_MODEL, ADA_TN), lambda l, j: (l, 0, j)),
            pl.BlockSpec((None, 1, ADA_TN), lambda l, j: (l, 0, j)),
        ],
        out_specs=pl.BlockSpec((None, BATCH, ADA_TN), lambda l, j: (l, 0, j)),
        out_shape=jax.ShapeDtypeStruct((DEPTH, BATCH, n), F32),
        compiler_params=_params(("arbitrary", "arbitrary")),
        name="ada_mod",
    )(c, w_ada, b_ada.reshape(DEPTH, 1, n))


def _rot_kernel(pos_ref, freq_ref, o_ref):
    ang = freq_ref[...] * pos_ref[...].astype(F32)
    zeros = jnp.zeros((DIFF_HEAD_DIM - ROT_DIM, SEQ), F32)
    slab = jnp.concatenate([jnp.cos(ang), jnp.sin(ang), zeros], axis=0)
    feat = jnp.concatenate([slab, slab], axis=0)
    for t0 in range(0, SEQ, LANES):
        o_ref[t0:t0 + LANES, :] = feat[:, t0:t0 + LANES].T


def _rot_call(positions):
    inv_freq = ROPE_THETA ** (-jnp.arange(0, ROT_DIM, 2, dtype=F32) / ROT_DIM)
    return pl.pallas_call(
        _rot_kernel,
        grid=(BATCH,),
        in_specs=[
            pl.BlockSpec((None, 1, SEQ), lambda b: (b, 0, 0)),
            pl.BlockSpec((ROT_HALF, 1), lambda b: (0, 0)),
        ],
        out_specs=pl.BlockSpec((SEQ, LANES), lambda b: (b, 0)),
        out_shape=jax.ShapeDtypeStruct((TOKENS, LANES), F32),
        compiler_params=_params(("arbitrary",)),
        name="rotary_table",
    )(positions.reshape(BATCH, 1, SEQ), inv_freq.reshape(ROT_HALF, 1))


def _mix_in_kernel(x_ref, shift_ref, scale_ref, g_ref, tab_ref, w_ref,
                   a_ref, q_ref, k_ref, vt_ref):
    h = _modulate(x_ref[...], g_ref[...], shift_ref[...], scale_ref[...]).astype(BF16)
    z = jnp.dot(h, w_ref[...], preferred_element_type=F32)
    a_ref[...] = z[:, :CONV_CH] * _sigmoid(z[:, CONV_CH:2 * CONV_CH])
    tab = tab_ref[...]
    dim = lax.broadcasted_iota(jnp.int32, (1, LANES), 1) % DIFF_HEAD_DIM
    lo, hi = dim < ROT_HALF, (dim >= ROT_HALF) & (dim < ROT_DIM)
    cmul = jnp.where(lo, tab, jnp.where(hi, pltpu.roll(tab, ROT_HALF, 1), 1.0))
    s_up = jnp.where(lo, -pltpu.roll(tab, LANES - ROT_HALF, 1), 0.0)
    s_dn = jnp.where(hi, tab, 0.0)
    q_off = 2 * CONV_CH
    k_off = q_off + DIFF_WIDTH
    for hd in range(DIFF_HEADS):
        for off, ref, mul in ((q_off, q_ref, Q_SCALE), (k_off, k_ref, None)):
            t = z[:, off + hd * LANES: off + (hd + 1) * LANES]
            r = (t * cmul + pltpu.roll(t, LANES - ROT_HALF, 1) * s_up
                 + pltpu.roll(t, ROT_HALF, 1) * s_dn)
            if mul is not None:
                r = r * mul
            ref[:, hd * LANES:(hd + 1) * LANES] = r.astype(BF16)
    v_off = k_off + DIFF_WIDTH
    for hd in range(DIFF_HEADS):
        r0 = hd * V_EXT
        vt_ref[r0:r0 + DIFF_V_DIM, :] = z[:, v_off + hd * LANES:v_off + (hd + 1) * LANES].T.astype(BF16)
        vt_ref[r0 + DIFF_V_DIM:r0 + V_EXT, :] = jnp.ones((ATT_ONES_ROWS, TM), BF16)


def _mix_in_call(x, mods, norm_g, tab, w_in, layer):
    tok = lambda w: pl.BlockSpec((TM, w), lambda i: (i, 0))
    return pl.pallas_call(
        _mix_in_kernel,
        grid=(N_TILES,),
        in_specs=[
            tok(D_MODEL),
            _mod_spec(layer, 1, 0),
            _mod_spec(layer, 1, 1),
            _pick_spec(norm_g.shape, layer, 1),
            tok(LANES),
            _pick_spec(w_in.shape, layer // 2),
        ],
        out_specs=[tok(CONV_CH), tok(DIFF_WIDTH), tok(DIFF_WIDTH),
                   pl.BlockSpec((None, DIFF_HEADS * V_EXT, TM),
                                lambda i: (i // TILES_PER_SEQ, 0, i % TILES_PER_SEQ))],
        out_shape=[
            jax.ShapeDtypeStruct((TOKENS, CONV_CH), F32),
            jax.ShapeDtypeStruct((TOKENS, DIFF_WIDTH), BF16),
            jax.ShapeDtypeStruct((TOKENS, DIFF_WIDTH), BF16),
            jax.ShapeDtypeStruct((BATCH, DIFF_HEADS * V_EXT, SEQ), BF16),
        ],
        compiler_params=_params(("arbitrary",)),
        name=f"mix_in_l{layer}",
    )(x, mods, mods, norm_g, tab, w_in)


def _attn_kernel(q_ref, k_ref, vt_ext, dl_ref, sg_ref, o_ref, *, lambda_init):
    dl = dl_ref[...]
    lam = (jnp.exp(jnp.sum(dl[0:1] * dl[1:2], axis=1, keepdims=True))
           - jnp.exp(jnp.sum(dl[2:3] * dl[3:4], axis=1, keepdims=True)) + lambda_init)
    first = lax.broadcasted_iota(jnp.int32, (1, LANES), 1) < DIFF_HEAD_DIM
    out_gain = sg_ref[...] * (1.0 - lambda_init)
    nt = (((1,), (1,)), ((), ()))
    visible = ((lax.broadcasted_iota(jnp.int32, (ATT_QBLK, 1), 0) // CHUNK)
               <= (lax.broadcasted_iota(jnp.int32, (1, 2 * ATT_QBLK), 1) % ATT_QBLK // CHUNK))

    def scores(blk):
        row0 = blk * ATT_QBLK
        klen = row0 + ATT_QBLK
        q = q_ref[row0:klen, :]
        q_both = jnp.concatenate([jnp.where(first, q, jnp.zeros_like(q)),
                                  jnp.where(first, jnp.zeros_like(q), q)], axis=0)
        return lax.dot_general(k_ref[0:klen, :], q_both, nt, preferred_element_type=F32)

    n_blk = SEQ // ATT_QBLK
    ahead = [scores(blk) for blk in range(ATT_AHEAD)]
    for blk in range(n_blk):
        row0 = blk * ATT_QBLK
        klen = row0 + ATT_QBLK
        s = ahead.pop(0)
        if blk + ATT_AHEAD < n_blk:
            ahead.append(scores(blk + ATT_AHEAD))
        s_diag = jnp.where(visible, s[row0:, :], -jnp.inf)
        m = jnp.max(s_diag, axis=0, keepdims=True)
        if row0:
            s_full = s[:row0, :]
            m = jnp.maximum(m, jnp.max(s_full, axis=0, keepdims=True))
        p = jnp.exp2(s_diag - m).astype(BF16)
        if row0:
            p = jnp.concatenate([jnp.exp2(s_full - m).astype(BF16), p], axis=0)
        pv = jnp.dot(vt_ext[:, 0:klen], p, preferred_element_type=F32)
        inv = 1.0 / pv[DIFF_V_DIM:DIFF_V_DIM + 1, :]
        o = (pv[:DIFF_V_DIM, :ATT_QBLK] * inv[:, :ATT_QBLK]
             - pv[:DIFF_V_DIM, ATT_QBLK:] * (lam * inv[:, ATT_QBLK:]))
        ms = jnp.mean(o * o, axis=0, keepdims=True)
        o_ref[row0:klen, :] = ((o * lax.rsqrt(ms + EPS)).T * out_gain).astype(BF16)


def _attn_call(q, k, vt, diff_lambda, subln_g, layer):
    lambda_init = 0.8 - 0.6 * float(np.exp(-0.3 * layer))
    head = lambda: pl.BlockSpec((None, SEQ, LANES), lambda b, h: (b, 0, h))
    shp = (BATCH, SEQ, DIFF_WIDTH)
    return pl.pallas_call(
        functools.partial(_attn_kernel, lambda_init=lambda_init),
        grid=(BATCH, DIFF_HEADS),
        in_specs=[
            head(), head(),
            pl.BlockSpec((None, V_EXT, SEQ), lambda b, h: (b, h, 0)),
            _pick_spec(diff_lambda.shape, layer // 2),
            _pick_spec(subln_g.shape, layer // 2),
        ],
        out_specs=head(),
        out_shape=jax.ShapeDtypeStruct(shp, BF16),
        compiler_params=_params(("arbitrary", "arbitrary")),
        name=f"diff_attn_l{layer}",
    )(q.reshape(shp), k.reshape(shp), vt, diff_lambda, subln_g)


def _conv_out_stage(x_ref, gate_ref, a_ref, halo_ref, o_ref, cw_ref, cb_ref, lg_ref, lb_ref, w_ref,
                    abuf, shifted, conv, wrep):
    seq_start = (pl.program_id(0) % TILES_PER_SEQ) == 0
    halo = halo_ref[...]
    abuf[0:HALO, :] = jnp.where(seq_start, jnp.zeros_like(halo), halo)
    abuf[HALO:, :] = a_ref[...]
    lead = HALO - (CONV_WIDTH - 1)
    span = CONV_WIN + HALO - SUBLANES
    for j in range(CONV_WIDTH):
        wrep[j] = jnp.broadcast_to(cw_ref[j:j + 1, :], (SUBLANES, CONV_CH))
    bias = jnp.broadcast_to(cb_ref[...], (SUBLANES, CONV_CH))
    groups = CONV_ROWS // SUBLANES
    for base in range(0, TM, CONV_WIN):
        for r in range(1, SUBLANES):
            shifted[r - 1, 0:span, :] = abuf[base + r:base + r + span, :]
        for r0 in range(0, CONV_WIN, CONV_ROWS):
            accs = [bias] * groups
            for j in range(CONV_WIDTH):
                whole, r = divmod(lead + j, SUBLANES)
                w = wrep[j]
                for grp in range(groups):
                    start = r0 + (whole + grp) * SUBLANES
                    if r == 0:
                        src = abuf[base + start:base + start + SUBLANES, :]
                    else:
                        src = shifted[r - 1, start:start + SUBLANES, :]
                    accs[grp] = accs[grp] + w * src
            for grp in range(groups):
                row = base + r0 + grp * SUBLANES
                conv[row:row + SUBLANES, :] = accs[grp]
    c = conv[...]
    mu = jnp.mean(c, axis=-1, keepdims=True)
    d = c - mu
    var = jnp.mean(d * d, axis=-1, keepdims=True)
    a = _silu(d * lax.rsqrt(var + EPS) * lg_ref[...] + lb_ref[...]).astype(BF16)
    y = (jnp.dot(a, w_ref[0:CONV_CH, :], preferred_element_type=F32)
         + jnp.dot(o_ref[...], w_ref[CONV_CH:, :], preferred_element_type=F32))
    return x_ref[...] + gate_ref[...] * y


def _pool_stage(x_ref, xh_ref, shift_ref, scale_ref, gate_ref, g_ref, pw_ref, ps_ref, hbuf, level):
    t_in_seq = (pl.program_id(0) % TILES_PER_SEQ) * TM
    x = x_ref[...]
    g, shift, scale = g_ref[...], shift_ref[...], scale_ref[...]
    hh = _modulate(xh_ref[...], g, shift, scale)
    hbuf[0:POOL_HALO, :] = jnp.where(t_in_seq == 0, jnp.zeros_like(hh), hh)
    hbuf[POOL_HALO:, :] = _modulate(x, g, shift, scale)
    pos = (t_in_seq + 1 + lax.broadcasted_iota(jnp.int32, (TM, 1), 0)).astype(F32)
    rows = POOL_HALO + TM
    ys = []
    for grp, win in enumerate(POOL_WINDOWS):
        lo, hi = grp * POOL_GROUP_CH, (grp + 1) * POOL_GROUP_CH
        cur = hbuf[POOL_HALO:, lo:hi]
        src = lambda r0, r1: hbuf[r0:r1, lo:hi]
        lag, k = 1, 0
        while 2 * lag < win:
            start = SUBLANES * (k + 1)
            level[k, start:rows, :] = src(start, rows) + src(start - lag, rows - lag)
            src = lambda r0, r1, k=k: level[k, r0:r1, :]
            lag, k = 2 * lag, k + 1
        tot = src(POOL_HALO, rows) + src(POOL_HALO - lag, rows - lag)
        p = (tot / jnp.minimum(pos, float(win)) - cur).astype(BF16)
        ys.append(jnp.dot(p, pw_ref[grp], preferred_element_type=F32))
    y = jnp.concatenate(ys, axis=-1) * ps_ref[...]
    return x + gate_ref[...] * y


def _mixer_kernel(*refs, stage, n_in):
    out_ref = refs[n_in]
    out_ref[...] = stage(*refs[:n_in], *refs[n_in + 1:])


def _tok_spec(width):
    return pl.BlockSpec((TM, width), lambda i: (i, 0))


def _halo_spec(rows, width):
    return pl.BlockSpec((rows, width), lambda i: (jnp.maximum(i * (TM // rows) - 1, 0), 0))


def _mixer_call(stage, args, specs, scratch, name):
    return pl.pallas_call(
        functools.partial(_mixer_kernel, stage=stage, n_in=len(args)),
        grid=(N_TILES,),
        in_specs=specs,
        out_specs=_tok_spec(D_MODEL),
        out_shape=jax.ShapeDtypeStruct((TOKENS, D_MODEL), F32),
        scratch_shapes=scratch,
        compiler_params=_params(("arbitrary",)),
        name=name,
    )(*args)


def _conv_out_call(x, mods, a, o, conv_w, conv_b, ln_g, ln_b, w_out, layer):
    e = layer // 2
    specs = [
        _tok_spec(D_MODEL),
        _mod_spec(layer, 1, 2),
        _tok_spec(CONV_CH),
        _halo_spec(HALO, CONV_CH),
        _tok_spec(DIFF_WIDTH),
        _pick_spec(conv_w.shape, e),
        _pick_spec(conv_b.shape, e),
        _pick_spec(ln_g.shape, e),
        _pick_spec(ln_b.shape, e),
        _pick_spec(w_out.shape, e),
    ]
    scratch = [pltpu.VMEM((HALO + TM, CONV_CH), F32),
               pltpu.VMEM((SUBLANES - 1, CONV_WIN + HALO - SUBLANES, CONV_CH), F32),
               pltpu.VMEM((TM, CONV_CH), F32),
               pltpu.VMEM((CONV_WIDTH, SUBLANES, CONV_CH), F32)]
    return _mixer_call(_conv_out_stage, [x, mods, a, a, o, conv_w, conv_b, ln_g, ln_b, w_out],
                       specs, scratch, f"mix_out_l{layer}")


def _pool_call(x, mods, norm_g, pool_w, pool_scale, layer):
    specs = [
        _tok_spec(D_MODEL),
        _halo_spec(POOL_HALO, D_MODEL),
        _mod_spec(layer, 1, 0),
        _mod_spec(layer, 1, 1),
        _mod_spec(layer, 1, 2),
        _pick_spec(norm_g.shape, layer, 1),
        _pick_spec(pool_w.shape, layer // 2),
        _pick_spec(pool_scale.shape, layer // 2),
    ]
    n_levels = int(np.log2(max(POOL_WINDOWS))) - 1
    assert SUBLANES * (n_levels + 1) <= POOL_HALO and max(POOL_WINDOWS) <= POOL_HALO
    scratch = [pltpu.VMEM((POOL_HALO + TM, D_MODEL), F32),
               pltpu.VMEM((n_levels, POOL_HALO + TM, POOL_GROUP_CH), F32)]
    return _mixer_call(_pool_stage, [x, x, mods, mods, mods, norm_g, pool_w, pool_scale],
                       specs, scratch, f"pool_l{layer}")


def _weight_copy(w_hbm, lead, chunk, rows, stage, sem, slot):
    src = w_hbm.at[lead[0], lead[1], pl.ds(chunk * rows, rows), :]
    return pltpu.make_async_copy(src, stage.at[slot], sem.at[slot])


def _load_weights_as_bf16(jobs):
    chunks, used = [], {}
    for w_hbm, lead, rows, stage, sem, dst in jobs:
        for c in range(dst.shape[0] // rows):
            slot = used.get(id(stage), 0) % 2
            used[id(stage)] = slot + 1
            copy = _weight_copy(w_hbm, lead, c, rows, stage, sem, slot)
            chunks.append((copy, stage, slot, dst, c * rows, rows))
    chunks[0][0].start()
    for n, (copy, stage, slot, dst, row0, rows) in enumerate(chunks):
        if n + 1 < len(chunks):
            chunks[n + 1][0].start()
        copy.wait()
        dst[row0:row0 + rows, :] = stage[slot].astype(BF16)


def _ffn_kernel(x_ref, shift_ref, scale_ref, gate_ref, g_ref, w1_hbm, w3_hbm, w2_hbm, fg_ref,
                o_ref, w1_ref, w3_ref, w2_ref, stage_up, stage_down, sem_up, sem_down,
                *, final, lead):
    @pl.when(pl.program_id(0) == 0)
    def _():
        _load_weights_as_bf16([
            (w1_hbm, lead, FFN_UP_CHUNK, stage_up, sem_up, w1_ref),
            (w3_hbm, lead, FFN_UP_CHUNK, stage_up, sem_up, w3_ref),
            (w2_hbm, lead, FFN_DOWN_CHUNK, stage_down, sem_down, w2_ref),
        ])

    for r0 in range(0, TM_FFN, FFN_SUB):
        x = x_ref[r0:r0 + FFN_SUB, :]
        h = _modulate(x, g_ref[...], shift_ref[...], scale_ref[...]).astype(BF16)
        a = jnp.dot(h, w1_ref[...], preferred_element_type=F32)
        b = jnp.dot(h, w3_ref[...], preferred_element_type=F32)
        u = (_silu(a) * b).astype(BF16)
        y = jnp.dot(u, w2_ref[...], preferred_element_type=F32)
        xo = x + (0.5 * gate_ref[...]) * y
        if final:
            ms = jnp.mean(xo * xo, axis=-1, keepdims=True)
            xo = xo * lax.rsqrt(ms + EPS) * fg_ref[...]
        o_ref[r0:r0 + FFN_SUB, :] = xo


def _ffn_call(x, mods, norm_g, w1, w3, w2, final_g, layer, sub, final):
    half = sub // 2
    in_hbm = pl.BlockSpec(memory_space=pl.ANY)
    return pl.pallas_call(
        functools.partial(_ffn_kernel, final=final, lead=(layer, half)),
        grid=(TOKENS // TM_FFN,),
        in_specs=[
            pl.BlockSpec((TM_FFN, D_MODEL), lambda i: (i, 0)),
            _mod_spec(layer, sub, 0, TM_FFN),
            _mod_spec(layer, sub, 1, TM_FFN),
            _mod_spec(layer, sub, 2, TM_FFN),
            _pick_spec(norm_g.shape, layer, sub),
            in_hbm, in_hbm, in_hbm,
            _pick_spec(final_g.shape),
        ],
        out_specs=pl.BlockSpec((TM_FFN, D_MODEL), lambda i: (i, 0)),
        out_shape=jax.ShapeDtypeStruct((TOKENS, D_MODEL), F32),
        scratch_shapes=[
            pltpu.VMEM((D_MODEL, D_FF), BF16), pltpu.VMEM((D_MODEL, D_FF), BF16),
            pltpu.VMEM((D_FF, D_MODEL), BF16),
            pltpu.VMEM((2, FFN_UP_CHUNK, D_FF), F32), pltpu.VMEM((2, FFN_DOWN_CHUNK, D_MODEL), F32),
            pltpu.SemaphoreType.DMA((2,)), pltpu.SemaphoreType.DMA((2,)),
        ],
        compiler_params=_params(("arbitrary",)),
        name=f"ffn_l{layer}_s{sub}",
    )(x, mods, mods, mods, norm_g, w1, w3, w2, final_g)


def kernel(x, c, positions, w_ada, b_ada, norm_g, ffn_w1, ffn_w3, ffn_w2, w_in, w_out, conv_w, conv_b,
           conv_ln_g, conv_ln_b, diff_lambda, subln_g, pool_w, pool_scale, final_g):
    mods = _ada_call(c, w_ada, b_ada).reshape(DEPTH * MOD_ROWS_PER_LAYER, 1, D_MODEL)
    tab = _rot_call(positions)
    assert D_MODEL % FFN_UP_CHUNK == 0 and D_FF % FFN_DOWN_CHUNK == 0
    w1, w3, w2 = ffn_w1, ffn_w3, ffn_w2
    w_in, w_out, pool_w = w_in.astype(BF16), w_out.astype(BF16), pool_w.astype(BF16)
    rows = lambda v: v.reshape(v.shape[:-1] + (1, v.shape[-1]))
    norm_g, final_g = rows(norm_g), rows(final_g)
    conv_b, conv_ln_g, conv_ln_b = rows(conv_b), rows(conv_ln_g), rows(conv_ln_b)
    subln_g, pool_scale = rows(subln_g), rows(pool_scale)
    ffn = (mods, norm_g, w1, w3, w2, final_g)
    xt = x.reshape(TOKENS, D_MODEL)
    for l in range(DEPTH):
        xt = _ffn_call(xt, *ffn, l, 0, False)
        if l % 2 == 0:
            a, q, k, vt = _mix_in_call(xt, mods, norm_g, tab, w_in, l)
            o = _attn_call(q, k, vt, diff_lambda, subln_g, l)
            xt = _conv_out_call(xt, mods, a, o.reshape(TOKENS, DIFF_WIDTH), conv_w, conv_b,
                                conv_ln_g, conv_ln_b, w_out, l)
        else:
            xt = _pool_call(xt, mods, norm_g, pool_w, pool_scale, l)
        xt = _ffn_call(xt, *ffn, l, 2, l == DEPTH - 1)
    return xt.reshape(BATCH, SEQ, D_MODEL)
```

```python
import functools

import numpy as np
import jax
import jax.numpy as jnp
from jax import lax
from jax.experimental import pallas as pl
from jax.experimental.pallas import tpu as pltpu

D_MODEL = 1024
BATCH = 8
SEQ = 2048
DEPTH = 4
CHUNK = 64
CONV_CH = D_MODEL // 2
CONV_WIDTH = 31
DIFF_HEADS = 4
DIFF_HEAD_DIM = 64
DIFF_V_DIM = 2 * DIFF_HEAD_DIM
DIFF_WIDTH = DIFF_HEADS * DIFF_V_DIM
IN_COLS = 2 * CONV_CH + 3 * DIFF_WIDTH
MIX_WIDTH = CONV_CH + DIFF_WIDTH
ROPE_THETA = 500000.0
ROT_DIM = DIFF_HEAD_DIM // 4
ROT_HALF = ROT_DIM // 2
POOL_WINDOWS = (2, 4, 8, 16)
POOL_GROUP_CH = D_MODEL // len(POOL_WINDOWS)
D_FF = 2816
N_SUB = 3
EPS = 1e-6

TOKENS = BATCH * SEQ
MOD_ROWS_PER_BATCH = N_SUB * 3
MOD_ROWS_PER_LAYER = BATCH * MOD_ROWS_PER_BATCH

V7X_VMEM_LIMIT_BYTES = 56 * 1024 * 1024
LANES = 128
SUBLANES = 8
HALO = 32
POOL_HALO = 32
CONV_WIN = 512
CONV_ROWS = 32

TM = 1024
TM_FFN = 1024
FFN_SUB = 128
FFN_UP_CHUNK = 128
FFN_DOWN_CHUNK = 352
FFN_STAGE_SLOTS = 4
N_TILES = TOKENS // TM
TILES_PER_SEQ = SEQ // TM
ADA_TN = 1536
ATT_QBLK = 256
ATT_ONES_ROWS = 16
V_EXT = DIFF_V_DIM + ATT_ONES_ROWS
ATT_AHEAD = 2
ATT_HEADS_PER_STEP = 2
Q_SCALE = DIFF_HEAD_DIM ** -0.5 * float(np.log2(np.e))

F32 = jnp.float32
BF16 = jnp.bfloat16


def _silu(v):
    return v * (1.0 / (1.0 + jnp.exp(-v)))


def _sigmoid(v):
    return 1.0 / (1.0 + jnp.exp(-v))


def _modulate(x, g, shift, scale):
    ms = jnp.mean(x * x, axis=-1, keepdims=True)
    return (x * lax.rsqrt(ms + EPS) * g) * (1.0 + scale) + shift


def _mod_spec(layer, sub, which, tile=TM):
    base = layer * MOD_ROWS_PER_LAYER + sub * 3 + which
    return pl.BlockSpec(
        (None, 1, D_MODEL),
        lambda i: (base + (i // (SEQ // tile)) * MOD_ROWS_PER_BATCH, 0, 0))


def _pick_spec(shape, *lead):
    tail = tuple(shape[len(lead):])
    return pl.BlockSpec((None,) * len(lead) + tail, lambda *_: tuple(lead) + (0,) * len(tail),
                        pipeline_mode=pl.Buffered(1))


def _params(sem):
    return pltpu.CompilerParams(dimension_semantics=sem, vmem_limit_bytes=V7X_VMEM_LIMIT_BYTES)


def _ada_kernel(c_ref, w_ref, b_ref, o_ref):
    cond = _silu(c_ref[...]).astype(BF16)
    o_ref[...] = jnp.dot(cond, w_ref[...].astype(BF16), preferred_element_type=F32) + b_ref[...]


def _ada_call(c, w_ada, b_ada):
    n = N_SUB * 3 * D_MODEL
    return pl.pallas_call(
        _ada_kernel,
        grid=(DEPTH, n // ADA_TN),
        in_specs=[
            pl.BlockSpec((BATCH, D_MODEL), lambda l, j: (0, 0)),
            pl.BlockSpec((None, D_MODEL, ADA_TN), lambda l, j: (l, 0, j)),
            pl.BlockSpec((None, 1, ADA_TN), lambda l, j: (l, 0, j)),
        ],
        out_specs=pl.BlockSpec((None, BATCH, ADA_TN), lambda l, j: (l, 0, j)),
        out_shape=jax.ShapeDtypeStruct((DEPTH, BATCH, n), F32),
        compiler_params=_params(("arbitrary", "arbitrary")),
        name="ada_mod",
    )(c, w_ada, b_ada.reshape(DEPTH, 1, n))


def _rot_kernel(pos_ref, freq_ref, o_ref):
    ang = freq_ref[...] * pos_ref[...].astype(F32)
    zeros = jnp.zeros((DIFF_HEAD_DIM - ROT_DIM, SEQ), F32)
    slab = jnp.concatenate([jnp.cos(ang), jnp.sin(ang), zeros], axis=0)
    feat = jnp.concatenate([slab, slab], axis=0)
    for t0 in range(0, SEQ, LANES):
        o_ref[t0:t0 + LANES, :] = feat[:, t0:t0 + LANES].T


def _rot_call(positions):
    inv_freq = ROPE_THETA ** (-jnp.arange(0, ROT_DIM, 2, dtype=F32) / ROT_DIM)
    return pl.pallas_call(
        _rot_kernel,
        grid=(BATCH,),
        in_specs=[
            pl.BlockSpec((None, 1, SEQ), lambda b: (b, 0, 0)),
            pl.BlockSpec((ROT_HALF, 1), lambda b: (0, 0)),
        ],
        out_specs=pl.BlockSpec((SEQ, LANES), lambda b: (b, 0)),
        out_shape=jax.ShapeDtypeStruct((TOKENS, LANES), F32),
        compiler_params=_params(("arbitrary",)),
        name="rotary_table",
    )(positions.reshape(BATCH, 1, SEQ), inv_freq.reshape(ROT_HALF, 1))


def _mix_in_kernel(x_ref, shift_ref, scale_ref, g_ref, tab_ref, w_ref,
                   a_ref, q_ref, k_ref, vt_ref):
    h = _modulate(x_ref[...], g_ref[...], shift_ref[...], scale_ref[...]).astype(BF16)
    z = jnp.dot(h, w_ref[...], preferred_element_type=F32)
    a_ref[...] = z[:, :CONV_CH] * _sigmoid(z[:, CONV_CH:2 * CONV_CH])
    tab = tab_ref[...]
    dim = lax.broadcasted_iota(jnp.int32, (1, LANES), 1) % DIFF_HEAD_DIM
    lo, hi = dim < ROT_HALF, (dim >= ROT_HALF) & (dim < ROT_DIM)
    cmul = jnp.where(lo, tab, jnp.where(hi, pltpu.roll(tab, ROT_HALF, 1), 1.0))
    s_up = jnp.where(lo, -pltpu.roll(tab, LANES - ROT_HALF, 1), 0.0)
    s_dn = jnp.where(hi, tab, 0.0)
    q_off = 2 * CONV_CH
    k_off = q_off + DIFF_WIDTH
    for hd in range(DIFF_HEADS):
        for off, ref, mul in ((q_off, q_ref, Q_SCALE), (k_off, k_ref, None)):
            t = z[:, off + hd * LANES: off + (hd + 1) * LANES]
            r = (t * cmul + pltpu.roll(t, LANES - ROT_HALF, 1) * s_up
                 + pltpu.roll(t, ROT_HALF, 1) * s_dn)
            if mul is not None:
                r = r * mul
            ref[:, hd * LANES:(hd + 1) * LANES] = r.astype(BF16)
    v_off = k_off + DIFF_WIDTH
    for hd in range(DIFF_HEADS):
        r0 = hd * V_EXT
        vt_ref[r0:r0 + DIFF_V_DIM, :] = z[:, v_off + hd * LANES:v_off + (hd + 1) * LANES].T.astype(BF16)
        vt_ref[r0 + DIFF_V_DIM:r0 + V_EXT, :] = jnp.ones((ATT_ONES_ROWS, TM), BF16)


def _mix_in_call(x, mods, norm_g, tab, w_in, layer):
    tok = lambda w: pl.BlockSpec((TM, w), lambda i: (i, 0))
    return pl.pallas_call(
        _mix_in_kernel,
        grid=(N_TILES,),
        in_specs=[
            tok(D_MODEL),
            _mod_spec(layer, 1, 0),
            _mod_spec(layer, 1, 1),
            _pick_spec(norm_g.shape, layer, 1),
            tok(LANES),
            _pick_spec(w_in.shape, layer // 2),
        ],
        out_specs=[tok(CONV_CH), tok(DIFF_WIDTH), tok(DIFF_WIDTH),
                   pl.BlockSpec((None, DIFF_HEADS * V_EXT, TM),
                                lambda i: (i // TILES_PER_SEQ, 0, i % TILES_PER_SEQ))],
        out_shape=[
            jax.ShapeDtypeStruct((TOKENS, CONV_CH), F32),
            jax.ShapeDtypeStruct((TOKENS, DIFF_WIDTH), BF16),
            jax.ShapeDtypeStruct((TOKENS, DIFF_WIDTH), BF16),
            jax.ShapeDtypeStruct((BATCH, DIFF_HEADS * V_EXT, SEQ), BF16),
        ],
        compiler_params=_params(("arbitrary",)),
        name=f"mix_in_l{layer}",
    )(x, mods, mods, norm_g, tab, w_in)


def _attn_kernel(q_ref, k_ref, vt_ext, dl_ref, sg_ref, o_ref, *, lambda_init):
    dl = dl_ref[...]
    lam = (jnp.exp(jnp.sum(dl[0:1] * dl[1:2], axis=1, keepdims=True))
           - jnp.exp(jnp.sum(dl[2:3] * dl[3:4], axis=1, keepdims=True)) + lambda_init)
    first = lax.broadcasted_iota(jnp.int32, (1, LANES), 1) < DIFF_HEAD_DIM
    out_gain = sg_ref[...] * (1.0 - lambda_init)
    nt = (((1,), (1,)), ((), ()))
    visible = ((lax.broadcasted_iota(jnp.int32, (ATT_QBLK, 1), 0) // CHUNK)
               <= (lax.broadcasted_iota(jnp.int32, (1, 2 * ATT_QBLK), 1) % ATT_QBLK // CHUNK))

    def scores(item):
        hd, blk = item
        row0 = blk * ATT_QBLK
        klen = row0 + ATT_QBLK
        lanes = slice(hd * LANES, (hd + 1) * LANES)
        q = q_ref[row0:klen, lanes]
        q_both = jnp.concatenate([jnp.where(first, q, jnp.zeros_like(q)),
                                  jnp.where(first, jnp.zeros_like(q), q)], axis=0)
        return lax.dot_general(k_ref[0:klen, lanes], q_both, nt,
                               preferred_element_type=F32)

    items = [(hd, blk) for hd in range(ATT_HEADS_PER_STEP) for blk in range(SEQ // ATT_QBLK)]
    ahead = [scores(item) for item in items[:ATT_AHEAD]]
    for n, (hd, blk) in enumerate(items):
        row0 = blk * ATT_QBLK
        klen = row0 + ATT_QBLK
        s = ahead.pop(0)
        if n + ATT_AHEAD < len(items):
            ahead.append(scores(items[n + ATT_AHEAD]))
        s_diag = jnp.where(visible, s[row0:, :], -jnp.inf)
        m = jnp.max(s_diag, axis=0, keepdims=True)
        if row0:
            s_full = s[:row0, :]
            m = jnp.maximum(m, jnp.max(s_full, axis=0, keepdims=True))
        p = jnp.exp2(s_diag - m).astype(BF16)
        if row0:
            p = jnp.concatenate([jnp.exp2(s_full - m).astype(BF16), p], axis=0)
        pv = jnp.dot(vt_ext[hd * V_EXT:(hd + 1) * V_EXT, 0:klen], p,
                     preferred_element_type=F32)
        inv = 1.0 / pv[DIFF_V_DIM:DIFF_V_DIM + 1, :]
        o = (pv[:DIFF_V_DIM, :ATT_QBLK] * inv[:, :ATT_QBLK]
             - pv[:DIFF_V_DIM, ATT_QBLK:] * (lam * inv[:, ATT_QBLK:]))
        ms = jnp.mean(o * o, axis=0, keepdims=True)
        o_ref[row0:klen, hd * LANES:(hd + 1) * LANES] = (
            (o * lax.rsqrt(ms + EPS)).T * out_gain).astype(BF16)


def _attn_call(q, k, vt, diff_lambda, subln_g, layer):
    lambda_init = 0.8 - 0.6 * float(np.exp(-0.3 * layer))
    head = lambda: pl.BlockSpec((None, SEQ, ATT_HEADS_PER_STEP * LANES), lambda b, h: (b, 0, h))
    shp = (BATCH, SEQ, DIFF_WIDTH)
    return pl.pallas_call(
        functools.partial(_attn_kernel, lambda_init=lambda_init),
        grid=(BATCH, DIFF_HEADS // ATT_HEADS_PER_STEP),
        in_specs=[
            head(), head(),
            pl.BlockSpec((None, ATT_HEADS_PER_STEP * V_EXT, SEQ), lambda b, h: (b, h, 0)),
            _pick_spec(diff_lambda.shape, layer // 2),
            _pick_spec(subln_g.shape, layer // 2),
        ],
        out_specs=head(),
        out_shape=jax.ShapeDtypeStruct(shp, BF16),
        compiler_params=_params(("arbitrary", "arbitrary")),
        name=f"diff_attn_l{layer}",
    )(q.reshape(shp), k.reshape(shp), vt, diff_lambda, subln_g)


def _conv_out_stage(x_ref, gate_ref, a_ref, halo_ref, o_ref, cw_ref, cb_ref, lg_ref, lb_ref, w_ref,
                    abuf, shifted, conv, wrep):
    seq_start = (pl.program_id(0) % TILES_PER_SEQ) == 0
    halo = halo_ref[...]
    abuf[0:HALO, :] = jnp.where(seq_start, jnp.zeros_like(halo), halo)
    abuf[HALO:, :] = a_ref[...]
    lead = HALO - (CONV_WIDTH - 1)
    span = CONV_WIN + HALO - SUBLANES
    for j in range(CONV_WIDTH):
        wrep[j] = jnp.broadcast_to(cw_ref[j:j + 1, :], (SUBLANES, CONV_CH))
    bias = jnp.broadcast_to(cb_ref[...], (SUBLANES, CONV_CH))
    groups = CONV_ROWS // SUBLANES
    for base in range(0, TM, CONV_WIN):
        for r in range(1, SUBLANES):
            shifted[r - 1, 0:span, :] = abuf[base + r:base + r + span, :]
        for r0 in range(0, CONV_WIN, CONV_ROWS):
            accs = [bias] * groups
            for j in range(CONV_WIDTH):
                whole, r = divmod(lead + j, SUBLANES)
                w = wrep[j]
                for grp in range(groups):
                    start = r0 + (whole + grp) * SUBLANES
                    if r == 0:
                        src = abuf[base + start:base + start + SUBLANES, :]
                    else:
                        src = shifted[r - 1, start:start + SUBLANES, :]
                    accs[grp] = accs[grp] + w * src
            for grp in range(groups):
                row = base + r0 + grp * SUBLANES
                conv[row:row + SUBLANES, :] = accs[grp]
    c = conv[...]
    mu = jnp.mean(c, axis=-1, keepdims=True)
    d = c - mu
    var = jnp.mean(d * d, axis=-1, keepdims=True)
    a = _silu(d * lax.rsqrt(var + EPS) * lg_ref[...] + lb_ref[...]).astype(BF16)
    y = (jnp.dot(a, w_ref[0:CONV_CH, :], preferred_element_type=F32)
         + jnp.dot(o_ref[...], w_ref[CONV_CH:, :], preferred_element_type=F32))
    return x_ref[...] + gate_ref[...] * y


def _pool_stage(x_ref, xh_ref, shift_ref, scale_ref, gate_ref, g_ref, pw_ref, ps_ref, hbuf, level):
    t_in_seq = (pl.program_id(0) % TILES_PER_SEQ) * TM
    x = x_ref[...]
    g, shift, scale = g_ref[...], shift_ref[...], scale_ref[...]
    hh = _modulate(xh_ref[...], g, shift, scale)
    hbuf[0:POOL_HALO, :] = jnp.where(t_in_seq == 0, jnp.zeros_like(hh), hh)
    hbuf[POOL_HALO:, :] = _modulate(x, g, shift, scale)
    pos = (t_in_seq + 1 + lax.broadcasted_iota(jnp.int32, (TM, 1), 0)).astype(F32)
    rows = POOL_HALO + TM
    ys = []
    for grp, win in enumerate(POOL_WINDOWS):
        lo, hi = grp * POOL_GROUP_CH, (grp + 1) * POOL_GROUP_CH
        cur = hbuf[POOL_HALO:, lo:hi]
        src = lambda r0, r1: hbuf[r0:r1, lo:hi]
        lag, k = 1, 0
        while 2 * lag < win:
            start = SUBLANES * (k + 1)
            level[k, start:rows, :] = src(start, rows) + src(start - lag, rows - lag)
            src = lambda r0, r1, k=k: level[k, r0:r1, :]
            lag, k = 2 * lag, k + 1
        tot = src(POOL_HALO, rows) + src(POOL_HALO - lag, rows - lag)
        p = (tot / jnp.minimum(pos, float(win)) - cur).astype(BF16)
        ys.append(jnp.dot(p, pw_ref[grp], preferred_element_type=F32))
    y = jnp.concatenate(ys, axis=-1) * ps_ref[...]
    return x + gate_ref[...] * y


def _mixer_kernel(*refs, stage, n_in):
    out_ref = refs[n_in]
    out_ref[...] = stage(*refs[:n_in], *refs[n_in + 1:])


def _tok_spec(width):
    return pl.BlockSpec((TM, width), lambda i: (i, 0))


def _halo_spec(rows, width):
    return pl.BlockSpec((rows, width), lambda i: (jnp.maximum(i * (TM // rows) - 1, 0), 0))


def _mixer_call(stage, args, specs, scratch, name):
    return pl.pallas_call(
        functools.partial(_mixer_kernel, stage=stage, n_in=len(args)),
        grid=(N_TILES,),
        in_specs=specs,
        out_specs=_tok_spec(D_MODEL),
        out_shape=jax.ShapeDtypeStruct((TOKENS, D_MODEL), F32),
        scratch_shapes=scratch,
        compiler_params=_params(("arbitrary",)),
        name=name,
    )(*args)


def _conv_out_call(x, mods, a, o, conv_w, conv_b, ln_g, ln_b, w_out, layer):
    e = layer // 2
    specs = [
        _tok_spec(D_MODEL),
        _mod_spec(layer, 1, 2),
        _tok_spec(CONV_CH),
        _halo_spec(HALO, CONV_CH),
        _tok_spec(DIFF_WIDTH),
        _pick_spec(conv_w.shape, e),
        _pick_spec(conv_b.shape, e),
        _pick_spec(ln_g.shape, e),
        _pick_spec(ln_b.shape, e),
        _pick_spec(w_out.shape, e),
    ]
    scratch = [pltpu.VMEM((HALO + TM, CONV_CH), F32),
               pltpu.VMEM((SUBLANES - 1, CONV_WIN + HALO - SUBLANES, CONV_CH), F32),
               pltpu.VMEM((TM, CONV_CH), F32),
               pltpu.VMEM((CONV_WIDTH, SUBLANES, CONV_CH), F32)]
    return _mixer_call(_conv_out_stage, [x, mods, a, a, o, conv_w, conv_b, ln_g, ln_b, w_out],
                       specs, scratch, f"mix_out_l{layer}")


def _pool_call(x, mods, norm_g, pool_w, pool_scale, layer):
    specs = [
        _tok_spec(D_MODEL),
        _halo_spec(POOL_HALO, D_MODEL),
        _mod_spec(layer, 1, 0),
        _mod_spec(layer, 1, 1),
        _mod_spec(layer, 1, 2),
        _pick_spec(norm_g.shape, layer, 1),
        _pick_spec(pool_w.shape, layer // 2),
        _pick_spec(pool_scale.shape, layer // 2),
    ]
    n_levels = int(np.log2(max(POOL_WINDOWS))) - 1
    assert SUBLANES * (n_levels + 1) <= POOL_HALO and max(POOL_WINDOWS) <= POOL_HALO
    scratch = [pltpu.VMEM((POOL_HALO + TM, D_MODEL), F32),
               pltpu.VMEM((n_levels, POOL_HALO + TM, POOL_GROUP_CH), F32)]
    return _mixer_call(_pool_stage, [x, x, mods, mods, mods, norm_g, pool_w, pool_scale],
                       specs, scratch, f"pool_l{layer}")


def _weight_copy(w_hbm, lead, chunk, rows, stage, sem, slot):
    src = w_hbm.at[lead[0], lead[1], pl.ds(chunk * rows, rows), :]
    return pltpu.make_async_copy(src, stage.at[slot], sem.at[slot])


def _load_weights_as_bf16(jobs):
    chunks, used = [], {}
    for w_hbm, lead, rows, stage, sem, dst in jobs:
        for c in range(dst.shape[0] // rows):
            slot = used.get(id(stage), 0) % FFN_STAGE_SLOTS
            used[id(stage)] = slot + 1
            copy = _weight_copy(w_hbm, lead, c, rows, stage, sem, slot)
            chunks.append((copy, stage, slot, dst, c * rows, rows))
    depth = FFN_STAGE_SLOTS - 1
    for n in range(min(depth, len(chunks))):
        chunks[n][0].start()
    for n, (copy, stage, slot, dst, row0, rows) in enumerate(chunks):
        if n + depth < len(chunks):
            chunks[n + depth][0].start()
        copy.wait()
        dst[row0:row0 + rows, :] = stage[slot].astype(BF16)


def _ffn_kernel(x_ref, shift_ref, scale_ref, gate_ref, g_ref, w1_hbm, w3_hbm, w2_hbm, fg_ref,
                o_ref, w1_ref, w3_ref, w2_ref, stage_up, stage_down, sem_up, sem_down,
                *, final, lead):
    @pl.when(pl.program_id(0) == 0)
    def _():
        _load_weights_as_bf16([
            (w1_hbm, lead, FFN_UP_CHUNK, stage_up, sem_up, w1_ref),
            (w3_hbm, lead, FFN_UP_CHUNK, stage_up, sem_up, w3_ref),
            (w2_hbm, lead, FFN_DOWN_CHUNK, stage_down, sem_down, w2_ref),
        ])

    for r0 in range(0, TM_FFN, FFN_SUB):
        x = x_ref[r0:r0 + FFN_SUB, :]
        h = _modulate(x, g_ref[...], shift_ref[...], scale_ref[...]).astype(BF16)
        a = jnp.dot(h, w1_ref[...], preferred_element_type=F32)
        b = jnp.dot(h, w3_ref[...], preferred_element_type=F32)
        u = (_silu(a) * b).astype(BF16)
        y = jnp.dot(u, w2_ref[...], preferred_element_type=F32)
        xo = x + (0.5 * gate_ref[...]) * y
        if final:
            ms = jnp.mean(xo * xo, axis=-1, keepdims=True)
            xo = xo * lax.rsqrt(ms + EPS) * fg_ref[...]
        o_ref[r0:r0 + FFN_SUB, :] = xo


def _ffn_call(x, mods, norm_g, w1, w3, w2, final_g, layer, sub, final):
    half = sub // 2
    in_hbm = pl.BlockSpec(memory_space=pl.ANY)
    return pl.pallas_call(
        functools.partial(_ffn_kernel, final=final, lead=(layer, half)),
        grid=(TOKENS // TM_FFN,),
        in_specs=[
            pl.BlockSpec((TM_FFN, D_MODEL), lambda i: (i, 0)),
            _mod_spec(layer, sub, 0, TM_FFN),
            _mod_spec(layer, sub, 1, TM_FFN),
            _mod_spec(layer, sub, 2, TM_FFN),
            _pick_spec(norm_g.shape, layer, sub),
            in_hbm, in_hbm, in_hbm,
            _pick_spec(final_g.shape),
        ],
        out_specs=pl.BlockSpec((TM_FFN, D_MODEL), lambda i: (i, 0)),
        out_shape=jax.ShapeDtypeStruct((TOKENS, D_MODEL), F32),
        scratch_shapes=[
            pltpu.VMEM((D_MODEL, D_FF), BF16), pltpu.VMEM((D_MODEL, D_FF), BF16),
            pltpu.VMEM((D_FF, D_MODEL), BF16),
            pltpu.VMEM((FFN_STAGE_SLOTS, FFN_UP_CHUNK, D_FF), F32),
            pltpu.VMEM((FFN_STAGE_SLOTS, FFN_DOWN_CHUNK, D_MODEL), F32),
            pltpu.SemaphoreType.DMA((FFN_STAGE_SLOTS,)), pltpu.SemaphoreType.DMA((FFN_STAGE_SLOTS,)),
        ],
        compiler_params=_params(("arbitrary",)),
        name=f"ffn_l{layer}_s{sub}",
    )(x, mods, mods, mods, norm_g, w1, w3, w2, final_g)


def kernel(x, c, positions, w_ada, b_ada, norm_g, ffn_w1, ffn_w3, ffn_w2, w_in, w_out, conv_w, conv_b,
           conv_ln_g, conv_ln_b, diff_lambda, subln_g, pool_w, pool_scale, final_g):
    mods = _ada_call(c, w_ada, b_ada).reshape(DEPTH * MOD_ROWS_PER_LAYER, 1, D_MODEL)
    tab = _rot_call(positions)
    assert D_MODEL % FFN_UP_CHUNK == 0 and D_FF % FFN_DOWN_CHUNK == 0
    w1, w3, w2 = ffn_w1, ffn_w3, ffn_w2
    w_in, w_out, pool_w = w_in.astype(BF16), w_out.astype(BF16), pool_w.astype(BF16)
    rows = lambda v: v.reshape(v.shape[:-1] + (1, v.shape[-1]))
    norm_g, final_g = rows(norm_g), rows(final_g)
    conv_b, conv_ln_g, conv_ln_b = rows(conv_b), rows(conv_ln_g), rows(conv_ln_b)
    subln_g, pool_scale = rows(subln_g), rows(pool_scale)
    ffn = (mods, norm_g, w1, w3, w2, final_g)
    xt = x.reshape(TOKENS, D_MODEL)
    for l in range(DEPTH):
        xt = _ffn_call(xt, *ffn, l, 0, False)
        if l % 2 == 0:
            a, q, k, vt = _mix_in_call(xt, mods, norm_g, tab, w_in, l)
            o = _attn_call(q, k, vt, diff_lambda, subln_g, l)
            xt = _conv_out_call(xt, mods, a, o.reshape(TOKENS, DIFF_WIDTH), conv_w, conv_b,
                                conv_ln_g, conv_ln_b, w_out, l)
        else:
            xt = _pool_call(xt, mods, norm_g, pool_w, pool_scale, l)
        xt = _ffn_call(xt, *ffn, l, 2, l == DEPTH - 1)
    return xt.reshape(BATCH, SEQ, D_MODEL)
```

```python
import functools

import numpy as np
import jax
import jax.numpy as jnp
from jax import lax
from jax.experimental import pallas as pl
from jax.experimental.pallas import tpu as pltpu

D_MODEL = 1024
BATCH = 8
SEQ = 2048
DEPTH = 4
CHUNK = 64
CONV_CH = D_MODEL // 2
CONV_WIDTH = 31
DIFF_HEADS = 4
DIFF_HEAD_DIM = 64
DIFF_V_DIM = 2 * DIFF_HEAD_DIM
DIFF_WIDTH = DIFF_HEADS * DIFF_V_DIM
IN_COLS = 2 * CONV_CH + 3 * DIFF_WIDTH
MIX_WIDTH = CONV_CH + DIFF_WIDTH
ROPE_THETA = 500000.0
ROT_DIM = DIFF_HEAD_DIM // 4
ROT_HALF = ROT_DIM // 2
POOL_WINDOWS = (2, 4, 8, 16)
POOL_GROUP_CH = D_MODEL // len(POOL_WINDOWS)
D_FF = 2816
N_SUB = 3
EPS = 1e-6

TOKENS = BATCH * SEQ
MOD_ROWS_PER_BATCH = N_SUB * 3
MOD_ROWS_PER_LAYER = BATCH * MOD_ROWS_PER_BATCH

V7X_VMEM_LIMIT_BYTES = 56 * 1024 * 1024
LANES = 128
SUBLANES = 8
HALO = 32
POOL_HALO = 32
CONV_WIN = 512
CONV_ROWS = 32

TM = 1024
TM_FFN = 1024
FFN_SUB = 128
N_TILES = TOKENS // TM
TILES_PER_SEQ = SEQ // TM
ADA_TN = 1536
ATT_QBLK = 256
ATT_ONES_ROWS = 16
V_EXT = DIFF_V_DIM + ATT_ONES_ROWS
ATT_AHEAD = 2
ATT_HEADS_PER_STEP = 4
Q_SCALE = DIFF_HEAD_DIM ** -0.5 * float(np.log2(np.e))

F32 = jnp.float32
BF16 = jnp.bfloat16


def _silu(v):
    return v * (1.0 / (1.0 + jnp.exp(-v)))


def _sigmoid(v):
    return 1.0 / (1.0 + jnp.exp(-v))


def _modulate(x, g, shift, scale):
    ms = jnp.mean(x * x, axis=-1, keepdims=True)
    return (x * lax.rsqrt(ms + EPS) * g) * (1.0 + scale) + shift


def _mod_spec(layer, sub, which, tile=TM):
    base = layer * MOD_ROWS_PER_LAYER + sub * 3 + which
    return pl.BlockSpec(
        (None, 1, D_MODEL),
        lambda i: (base + (i // (SEQ // tile)) * MOD_ROWS_PER_BATCH, 0, 0))


def _pick_spec(shape, *lead):
    tail = tuple(shape[len(lead):])
    return pl.BlockSpec((None,) * len(lead) + tail, lambda *_: tuple(lead) + (0,) * len(tail),
                        pipeline_mode=pl.Buffered(1))


def _params(sem):
    return pltpu.CompilerParams(dimension_semantics=sem, vmem_limit_bytes=V7X_VMEM_LIMIT_BYTES)


def _ada_kernel(c_ref, w_ref, b_ref, o_ref):
    cond = _silu(c_ref[...]).astype(BF16)
    o_ref[...] = jnp.dot(cond, w_ref[...].astype(BF16), preferred_element_type=F32) + b_ref[...]


def _ada_call(c, w_ada, b_ada):
    n = N_SUB * 3 * D_MODEL
    return pl.pallas_call(
        _ada_kernel,
        grid=(DEPTH, n // ADA_TN),
        in_specs=[
            pl.BlockSpec((BATCH, D_MODEL), lambda l, j: (0, 0)),
            pl.BlockSpec((None, D_MODEL, ADA_TN), lambda l, j: (l, 0, j)),
            pl.BlockSpec((None, 1, ADA_TN), lambda l, j: (l, 0, j)),
        ],
        out_specs=pl.BlockSpec((None, BATCH, ADA_TN), lambda l, j: (l, 0, j)),
        out_shape=jax.ShapeDtypeStruct((DEPTH, BATCH, n), F32),
        compiler_params=_params(("arbitrary", "arbitrary")),
        name="ada_mod",
    )(c, w_ada, b_ada.reshape(DEPTH, 1, n))


def _rot_kernel(pos_ref, freq_ref, o_ref):
    ang = freq_ref[...] * pos_ref[...].astype(F32)
    zeros = jnp.zeros((DIFF_HEAD_DIM - ROT_DIM, SEQ), F32)
    slab = jnp.concatenate([jnp.cos(ang), jnp.sin(ang), zeros], axis=0)
    feat = jnp.concatenate([slab, slab], axis=0)
    for t0 in range(0, SEQ, LANES):
        o_ref[t0:t0 + LANES, :] = feat[:, t0:t0 + LANES].T


def _rot_call(positions):
    inv_freq = ROPE_THETA ** (-jnp.arange(0, ROT_DIM, 2, dtype=F32) / ROT_DIM)
    return pl.pallas_call(
        _rot_kernel,
        grid=(BATCH,),
        in_specs=[
            pl.BlockSpec((None, 1, SEQ), lambda b: (b, 0, 0)),
            pl.BlockSpec((ROT_HALF, 1), lambda b: (0, 0)),
        ],
        out_specs=pl.BlockSpec((SEQ, LANES), lambda b: (b, 0)),
        out_shape=jax.ShapeDtypeStruct((TOKENS, LANES), F32),
        compiler_params=_params(("arbitrary",)),
        name="rotary_table",
    )(positions.reshape(BATCH, 1, SEQ), inv_freq.reshape(ROT_HALF, 1))


def _mix_in_kernel(x_ref, shift_ref, scale_ref, g_ref, tab_ref, w_ref,
                   a_ref, q_ref, k_ref, vt_ref):
    h = _modulate(x_ref[...], g_ref[...], shift_ref[...], scale_ref[...]).astype(BF16)
    z = jnp.dot(h, w_ref[...], preferred_element_type=F32)
    a_ref[...] = z[:, :CONV_CH] * _sigmoid(z[:, CONV_CH:2 * CONV_CH])
    tab = tab_ref[...]
    dim = lax.broadcasted_iota(jnp.int32, (1, LANES), 1) % DIFF_HEAD_DIM
    lo, hi = dim < ROT_HALF, (dim >= ROT_HALF) & (dim < ROT_DIM)
    cmul = jnp.where(lo, tab, jnp.where(hi, pltpu.roll(tab, ROT_HALF, 1), 1.0))
    s_up = jnp.where(lo, -pltpu.roll(tab, LANES - ROT_HALF, 1), 0.0)
    s_dn = jnp.where(hi, tab, 0.0)
    q_off = 2 * CONV_CH
    k_off = q_off + DIFF_WIDTH
    for hd in range(DIFF_HEADS):
        for off, ref, mul in ((q_off, q_ref, Q_SCALE), (k_off, k_ref, None)):
            t = z[:, off + hd * LANES: off + (hd + 1) * LANES]
            r = (t * cmul + pltpu.roll(t, LANES - ROT_HALF, 1) * s_up
                 + pltpu.roll(t, ROT_HALF, 1) * s_dn)
            if mul is not None:
                r = r * mul
            ref[:, hd * LANES:(hd + 1) * LANES] = r.astype(BF16)
    v_off = k_off + DIFF_WIDTH
    for hd in range(DIFF_HEADS):
        r0 = hd * V_EXT
        vt_ref[r0:r0 + DIFF_V_DIM, :] = z[:, v_off + hd * LANES:v_off + (hd + 1) * LANES].T.astype(BF16)
        vt_ref[r0 + DIFF_V_DIM:r0 + V_EXT, :] = jnp.ones((ATT_ONES_ROWS, TM), BF16)


def _mix_in_call(x, mods, norm_g, tab, w_in, layer):
    tok = lambda w: pl.BlockSpec((TM, w), lambda i: (i, 0))
    return pl.pallas_call(
        _mix_in_kernel,
        grid=(N_TILES,),
        in_specs=[
            tok(D_MODEL),
            _mod_spec(layer, 1, 0),
            _mod_spec(layer, 1, 1),
            _pick_spec(norm_g.shape, layer, 1),
            tok(LANES),
            _pick_spec(w_in.shape, layer // 2),
        ],
        out_specs=[tok(CONV_CH), tok(DIFF_WIDTH), tok(DIFF_WIDTH),
                   pl.BlockSpec((None, DIFF_HEADS * V_EXT, TM),
                                lambda i: (i // TILES_PER_SEQ, 0, i % TILES_PER_SEQ))],
        out_shape=[
            jax.ShapeDtypeStruct((TOKENS, CONV_CH), F32),
            jax.ShapeDtypeStruct((TOKENS, DIFF_WIDTH), BF16),
            jax.ShapeDtypeStruct((TOKENS, DIFF_WIDTH), BF16),
            jax.ShapeDtypeStruct((BATCH, DIFF_HEADS * V_EXT, SEQ), BF16),
        ],
        compiler_params=_params(("arbitrary",)),
        name=f"mix_in_l{layer}",
    )(x, mods, mods, norm_g, tab, w_in)


def _attn_kernel(q_ref, k_ref, vt_ext, dl_ref, sg_ref, o_ref, *, lambda_init):
    dl = dl_ref[...]
    lam = (jnp.exp(jnp.sum(dl[0:1] * dl[1:2], axis=1, keepdims=True))
           - jnp.exp(jnp.sum(dl[2:3] * dl[3:4], axis=1, keepdims=True)) + lambda_init)
    first = lax.broadcasted_iota(jnp.int32, (1, LANES), 1) < DIFF_HEAD_DIM
    out_gain = sg_ref[...] * (1.0 - lambda_init)
    nt = (((1,), (1,)), ((), ()))
    visible = ((lax.broadcasted_iota(jnp.int32, (ATT_QBLK, 1), 0) // CHUNK)
               <= (lax.broadcasted_iota(jnp.int32, (1, 2 * ATT_QBLK), 1) % ATT_QBLK // CHUNK))

    def scores(item):
        hd, blk = item
        row0 = blk * ATT_QBLK
        klen = row0 + ATT_QBLK
        lanes = slice(hd * LANES, (hd + 1) * LANES)
        q = q_ref[row0:klen, lanes]
        q_both = jnp.concatenate([jnp.where(first, q, jnp.zeros_like(q)),
                                  jnp.where(first, jnp.zeros_like(q), q)], axis=0)
        return lax.dot_general(k_ref[0:klen, lanes], q_both, nt,
                               preferred_element_type=F32)

    items = [(hd, blk) for hd in range(ATT_HEADS_PER_STEP) for blk in range(SEQ // ATT_QBLK)]
    ahead = [scores(item) for item in items[:ATT_AHEAD]]
    for n, (hd, blk) in enumerate(items):
        row0 = blk * ATT_QBLK
        klen = row0 + ATT_QBLK
        s = ahead.pop(0)
        if n + ATT_AHEAD < len(items):
            ahead.append(scores(items[n + ATT_AHEAD]))
        s_diag = jnp.where(visible, s[row0:, :], -jnp.inf)
        m = jnp.max(s_diag, axis=0, keepdims=True)
        if row0:
            s_full = s[:row0, :]
            m = jnp.maximum(m, jnp.max(s_full, axis=0, keepdims=True))
        p = jnp.exp2(s_diag - m).astype(BF16)
        if row0:
            p = jnp.concatenate([jnp.exp2(s_full - m).astype(BF16), p], axis=0)
        pv = jnp.dot(vt_ext[hd * V_EXT:(hd + 1) * V_EXT, 0:klen], p,
                     preferred_element_type=F32)
        inv = 1.0 / pv[DIFF_V_DIM:DIFF_V_DIM + 1, :]
        o = (pv[:DIFF_V_DIM, :ATT_QBLK] * inv[:, :ATT_QBLK]
             - pv[:DIFF_V_DIM, ATT_QBLK:] * (lam * inv[:, ATT_QBLK:]))
        ms = jnp.mean(o * o, axis=0, keepdims=True)
        o_ref[row0:klen, hd * LANES:(hd + 1) * LANES] = (
            (o * lax.rsqrt(ms + EPS)).T * out_gain).astype(BF16)


def _attn_call(q, k, vt, diff_lambda, subln_g, layer):
    lambda_init = 0.8 - 0.6 * float(np.exp(-0.3 * layer))
    head = lambda: pl.BlockSpec((None, SEQ, ATT_HEADS_PER_STEP * LANES), lambda b, h: (b, 0, h))
    shp = (BATCH, SEQ, DIFF_WIDTH)
    return pl.pallas_call(
        functools.partial(_attn_kernel, lambda_init=lambda_init),
        grid=(BATCH, DIFF_HEADS // ATT_HEADS_PER_STEP),
        in_specs=[
            head(), head(),
            pl.BlockSpec((None, ATT_HEADS_PER_STEP * V_EXT, SEQ), lambda b, h: (b, h, 0)),
            _pick_spec(diff_lambda.shape, layer // 2),
            _pick_spec(subln_g.shape, layer // 2),
        ],
        out_specs=head(),
        out_shape=jax.ShapeDtypeStruct(shp, BF16),
        compiler_params=_params(("arbitrary", "arbitrary")),
        name=f"diff_attn_l{layer}",
    )(q.reshape(shp), k.reshape(shp), vt, diff_lambda, subln_g)


def _conv_out_stage(x_ref, gate_ref, a_ref, halo_ref, o_ref, cw_ref, cb_ref, lg_ref, lb_ref, w_ref,
                    abuf, shifted, conv, wrep):
    seq_start = (pl.program_id(0) % TILES_PER_SEQ) == 0
    halo = halo_ref[...]
    abuf[0:HALO, :] = jnp.where(seq_start, jnp.zeros_like(halo), halo)
    abuf[HALO:, :] = a_ref[...]
    lead = HALO - (CONV_WIDTH - 1)
    span = CONV_WIN + HALO - SUBLANES
    for j in range(CONV_WIDTH):
        wrep[j] = jnp.broadcast_to(cw_ref[j:j + 1, :], (SUBLANES, CONV_CH))
    bias = jnp.broadcast_to(cb_ref[...], (SUBLANES, CONV_CH))
    groups = CONV_ROWS // SUBLANES
    for base in range(0, TM, CONV_WIN):
        for r in range(1, SUBLANES):
            shifted[r - 1, 0:span, :] = abuf[base + r:base + r + span, :]
        for r0 in range(0, CONV_WIN, CONV_ROWS):
            accs = [bias] * groups
            for j in range(CONV_WIDTH):
                whole, r = divmod(lead + j, SUBLANES)
                w = wrep[j]
                for grp in range(groups):
                    start = r0 + (whole + grp) * SUBLANES
                    if r == 0:
                        src = abuf[base + start:base + start + SUBLANES, :]
                    else:
                        src = shifted[r - 1, start:start + SUBLANES, :]
                    accs[grp] = accs[grp] + w * src
            for grp in range(groups):
                row = base + r0 + grp * SUBLANES
                conv[row:row + SUBLANES, :] = accs[grp]
    c = conv[...]
    mu = jnp.mean(c, axis=-1, keepdims=True)
    d = c - mu
    var = jnp.mean(d * d, axis=-1, keepdims=True)
    a = _silu(d * lax.rsqrt(var + EPS) * lg_ref[...] + lb_ref[...]).astype(BF16)
    y = (jnp.dot(a, w_ref[0:CONV_CH, :], preferred_element_type=F32)
         + jnp.dot(o_ref[...], w_ref[CONV_CH:, :], preferred_element_type=F32))
    return x_ref[...] + gate_ref[...] * y


def _pool_stage(x_ref, xh_ref, shift_ref, scale_ref, gate_ref, g_ref, pw_ref, ps_ref, hbuf, level):
    t_in_seq = (pl.program_id(0) % TILES_PER_SEQ) * TM
    x = x_ref[...]
    g, shift, scale = g_ref[...], shift_ref[...], scale_ref[...]
    hh = _modulate(xh_ref[...], g, shift, scale)
    hbuf[0:POOL_HALO, :] = jnp.where(t_in_seq == 0, jnp.zeros_like(hh), hh)
    hbuf[POOL_HALO:, :] = _modulate(x, g, shift, scale)
    pos = (t_in_seq + 1 + lax.broadcasted_iota(jnp.int32, (TM, 1), 0)).astype(F32)
    rows = POOL_HALO + TM
    ys = []
    for grp, win in enumerate(POOL_WINDOWS):
        lo, hi = grp * POOL_GROUP_CH, (grp + 1) * POOL_GROUP_CH
        cur = hbuf[POOL_HALO:, lo:hi]
        src = lambda r0, r1: hbuf[r0:r1, lo:hi]
        lag, k = 1, 0
        while 2 * lag < win:
            start = SUBLANES * (k + 1)
            level[k, start:rows, :] = src(start, rows) + src(start - lag, rows - lag)
            src = lambda r0, r1, k=k: level[k, r0:r1, :]
            lag, k = 2 * lag, k + 1
        tot = src(POOL_HALO, rows) + src(POOL_HALO - lag, rows - lag)
        p = (tot / jnp.minimum(pos, float(win)) - cur).astype(BF16)
        ys.append(jnp.dot(p, pw_ref[grp], preferred_element_type=F32))
    y = jnp.concatenate(ys, axis=-1) * ps_ref[...]
    return x + gate_ref[...] * y


def _mixer_kernel(*refs, stage, n_in):
    out_ref = refs[n_in]
    out_ref[...] = stage(*refs[:n_in], *refs[n_in + 1:])


def _tok_spec(width):
    return pl.BlockSpec((TM, width), lambda i: (i, 0))


def _halo_spec(rows, width):
    return pl.BlockSpec((rows, width), lambda i: (jnp.maximum(i * (TM // rows) - 1, 0), 0))


def _mixer_call(stage, args, specs, scratch, name):
    return pl.pallas_call(
        functools.partial(_mixer_kernel, stage=stage, n_in=len(args)),
        grid=(N_TILES,),
        in_specs=specs,
        out_specs=_tok_spec(D_MODEL),
        out_shape=jax.ShapeDtypeStruct((TOKENS, D_MODEL), F32),
        scratch_shapes=scratch,
        compiler_params=_params(("arbitrary",)),
        name=name,
    )(*args)


def _conv_out_call(x, mods, a, o, conv_w, conv_b, ln_g, ln_b, w_out, layer):
    e = layer // 2
    specs = [
        _tok_spec(D_MODEL),
        _mod_spec(layer, 1, 2),
        _tok_spec(CONV_CH),
        _halo_spec(HALO, CONV_CH),
        _tok_spec(DIFF_WIDTH),
        _pick_spec(conv_w.shape, e),
        _pick_spec(conv_b.shape, e),
        _pick_spec(ln_g.shape, e),
        _pick_spec(ln_b.shape, e),
        _pick_spec(w_out.shape, e),
    ]
    scratch = [pltpu.VMEM((HALO + TM, CONV_CH), F32),
               pltpu.VMEM((SUBLANES - 1, CONV_WIN + HALO - SUBLANES, CONV_CH), F32),
               pltpu.VMEM((TM, CONV_CH), F32),
               pltpu.VMEM((CONV_WIDTH, SUBLANES, CONV_CH), F32)]
    return _mixer_call(_conv_out_stage, [x, mods, a, a, o, conv_w, conv_b, ln_g, ln_b, w_out],
                       specs, scratch, f"mix_out_l{layer}")


def _pool_call(x, mods, norm_g, pool_w, pool_scale, layer):
    specs = [
        _tok_spec(D_MODEL),
        _halo_spec(POOL_HALO, D_MODEL),
        _mod_spec(layer, 1, 0),
        _mod_spec(layer, 1, 1),
        _mod_spec(layer, 1, 2),
        _pick_spec(norm_g.shape, layer, 1),
        _pick_spec(pool_w.shape, layer // 2),
        _pick_spec(pool_scale.shape, layer // 2),
    ]
    n_levels = int(np.log2(max(POOL_WINDOWS))) - 1
    assert SUBLANES * (n_levels + 1) <= POOL_HALO and max(POOL_WINDOWS) <= POOL_HALO
    scratch = [pltpu.VMEM((POOL_HALO + TM, D_MODEL), F32),
               pltpu.VMEM((n_levels, POOL_HALO + TM, POOL_GROUP_CH), F32)]
    return _mixer_call(_pool_stage, [x, x, mods, mods, mods, norm_g, pool_w, pool_scale],
                       specs, scratch, f"pool_l{layer}")


def _ffn_kernel(x_ref, shift_ref, scale_ref, gate_ref, g_ref, w1_ref, w3_ref, w2_ref, fg_ref,
                *rest, final, round_next):
    if round_next:
        n1_ref, n3_ref, n2_ref, o_ref, r1_ref, r3_ref, r2_ref = rest
        r1_ref[...] = n1_ref[...].astype(BF16)
        r3_ref[...] = n3_ref[...].astype(BF16)
        r2_ref[...] = n2_ref[...].astype(BF16)
    else:
        (o_ref,) = rest

    for r0 in range(0, TM_FFN, FFN_SUB):
        x = x_ref[r0:r0 + FFN_SUB, :]
        h = _modulate(x, g_ref[...], shift_ref[...], scale_ref[...]).astype(BF16)
        a = jnp.dot(h, w1_ref[...], preferred_element_type=F32)
        b = jnp.dot(h, w3_ref[...], preferred_element_type=F32)
        u = (_silu(a) * b).astype(BF16)
        y = jnp.dot(u, w2_ref[...], preferred_element_type=F32)
        xo = x + (0.5 * gate_ref[...]) * y
        if final:
            ms = jnp.mean(xo * xo, axis=-1, keepdims=True)
            xo = xo * lax.rsqrt(ms + EPS) * fg_ref[...]
        o_ref[r0:r0 + FFN_SUB, :] = xo


def _ffn_call(x, mods, norm_g, weights, final_g, layer, sub, final, next_f32):
    steps = TOKENS // TM_FFN
    up_rows, down_rows = D_MODEL // steps, D_FF // steps
    in_specs = [
        pl.BlockSpec((TM_FFN, D_MODEL), lambda i: (i, 0)),
        _mod_spec(layer, sub, 0, TM_FFN),
        _mod_spec(layer, sub, 1, TM_FFN),
        _mod_spec(layer, sub, 2, TM_FFN),
        _pick_spec(norm_g.shape, layer, sub),
        _pick_spec(weights[0].shape),
        _pick_spec(weights[1].shape),
        _pick_spec(weights[2].shape),
        _pick_spec(final_g.shape),
    ]
    args = [x, mods, mods, mods, norm_g, *weights, final_g]
    out_specs = [pl.BlockSpec((TM_FFN, D_MODEL), lambda i: (i, 0))]
    out_shape = [jax.ShapeDtypeStruct((TOKENS, D_MODEL), F32)]
    if next_f32 is not None:
        nw1, nw3, nw2, nl, nh = next_f32
        slab = lambda rows, cols: pl.BlockSpec((None, None, rows, cols), lambda i: (nl, nh, i, 0))
        in_specs += [slab(up_rows, D_FF), slab(up_rows, D_FF), slab(down_rows, D_MODEL)]
        args += [nw1, nw3, nw2]
        out_specs += [pl.BlockSpec((up_rows, D_FF), lambda i: (i, 0)),
                      pl.BlockSpec((up_rows, D_FF), lambda i: (i, 0)),
                      pl.BlockSpec((down_rows, D_MODEL), lambda i: (i, 0))]
        out_shape += [jax.ShapeDtypeStruct((D_MODEL, D_FF), BF16),
                      jax.ShapeDtypeStruct((D_MODEL, D_FF), BF16),
                      jax.ShapeDtypeStruct((D_FF, D_MODEL), BF16)]
    outs = pl.pallas_call(
        functools.partial(_ffn_kernel, final=final, round_next=next_f32 is not None),
        grid=(steps,),
        in_specs=in_specs,
        out_specs=out_specs,
        out_shape=out_shape,
        compiler_params=_params(("arbitrary",)),
        name=f"ffn_l{layer}_s{sub}",
    )(*args)
    return outs[0], tuple(outs[1:])


def kernel(x, c, positions, w_ada, b_ada, norm_g, ffn_w1, ffn_w3, ffn_w2, w_in, w_out, conv_w, conv_b,
           conv_ln_g, conv_ln_b, diff_lambda, subln_g, pool_w, pool_scale, final_g):
    mods = _ada_call(c, w_ada, b_ada).reshape(DEPTH * MOD_ROWS_PER_LAYER, 1, D_MODEL)
    tab = _rot_call(positions)
    w_in, w_out, pool_w = w_in.astype(BF16), w_out.astype(BF16), pool_w.astype(BF16)
    rows = lambda v: v.reshape(v.shape[:-1] + (1, v.shape[-1]))
    norm_g, final_g = rows(norm_g), rows(final_g)
    conv_b, conv_ln_g, conv_ln_b = rows(conv_b), rows(conv_ln_g), rows(conv_ln_b)
    subln_g, pool_scale = rows(subln_g), rows(pool_scale)
    weights = tuple(w[0, 0].astype(BF16) for w in (ffn_w1, ffn_w3, ffn_w2))

    def half_step(xt, weights, l, sub):
        nl, nh = (l, 1) if sub == 0 else (l + 1, 0)
        next_f32 = (ffn_w1, ffn_w3, ffn_w2, nl, nh) if nl < DEPTH else None
        return _ffn_call(xt, mods, norm_g, weights, final_g, l, sub,
                         sub == 2 and l == DEPTH - 1, next_f32)

    xt = x.reshape(TOKENS, D_MODEL)
    for l in range(DEPTH):
        xt, weights = half_step(xt, weights, l, 0)
        if l % 2 == 0:
            a, q, k, vt = _mix_in_call(xt, mods, norm_g, tab, w_in, l)
            o = _attn_call(q, k, vt, diff_lambda, subln_g, l)
            xt = _conv_out_call(xt, mods, a, o.reshape(TOKENS, DIFF_WIDTH), conv_w, conv_b,
                                conv_ln_g, conv_ln_b, w_out, l)
        else:
            xt = _pool_call(xt, mods, norm_g, pool_w, pool_scale, l)
        xt, weights = half_step(xt, weights, l, 2)
    return xt.reshape(BATCH, SEQ, D_MODEL)
```

```python
import functools

import numpy as np
import jax
import jax.numpy as jnp
from jax import lax
from jax.experimental import pallas as pl
from jax.experimental.pallas import tpu as pltpu

D_MODEL = 1024
BATCH = 8
SEQ = 2048
DEPTH = 4
CHUNK = 64
CONV_CH = D_MODEL // 2
CONV_WIDTH = 31
DIFF_HEADS = 4
DIFF_HEAD_DIM = 64
DIFF_V_DIM = 2 * DIFF_HEAD_DIM
DIFF_WIDTH = DIFF_HEADS * DIFF_V_DIM
IN_COLS = 2 * CONV_CH + 3 * DIFF_WIDTH
MIX_WIDTH = CONV_CH + DIFF_WIDTH
ROPE_THETA = 500000.0
ROT_DIM = DIFF_HEAD_DIM // 4
ROT_HALF = ROT_DIM // 2
POOL_WINDOWS = (2, 4, 8, 16)
POOL_GROUP_CH = D_MODEL // len(POOL_WINDOWS)
D_FF = 2816
N_SUB = 3
EPS = 1e-6

TOKENS = BATCH * SEQ
MOD_ROWS_PER_BATCH = N_SUB * 3
MOD_ROWS_PER_LAYER = BATCH * MOD_ROWS_PER_BATCH

V7X_VMEM_LIMIT_BYTES = 56 * 1024 * 1024
LANES = 128
SUBLANES = 8
HALO = 32
POOL_HALO = 32
CONV_WIN = 512
CONV_ROWS = 32

TM = 1024
MIX_SUB = 256
TM_FFN = 1024
FFN_SUB = 128
N_TILES = TOKENS // TM
TILES_PER_SEQ = SEQ // TM
ADA_TN = 1536
ATT_QBLK = 256
ATT_ONES_ROWS = 16
V_EXT = DIFF_V_DIM + ATT_ONES_ROWS
ATT_AHEAD = 2
ATT_HEADS_PER_STEP = 4
Q_SCALE = DIFF_HEAD_DIM ** -0.5 * float(np.log2(np.e))

F32 = jnp.float32
BF16 = jnp.bfloat16


def _silu(v):
    return v * (1.0 / (1.0 + jnp.exp(-v)))


def _sigmoid(v):
    return 1.0 / (1.0 + jnp.exp(-v))


def _modulate(x, g, shift, scale):
    ms = jnp.mean(x * x, axis=-1, keepdims=True)
    return (x * lax.rsqrt(ms + EPS) * g) * (1.0 + scale) + shift


def _mod_spec(layer, sub, which, tile=TM):
    base = layer * MOD_ROWS_PER_LAYER + sub * 3 + which
    return pl.BlockSpec(
        (None, 1, D_MODEL),
        lambda i: (base + (i // (SEQ // tile)) * MOD_ROWS_PER_BATCH, 0, 0))


def _pick_spec(shape, *lead):
    tail = tuple(shape[len(lead):])
    return pl.BlockSpec((None,) * len(lead) + tail, lambda *_: tuple(lead) + (0,) * len(tail),
                        pipeline_mode=pl.Buffered(1))


def _round_specs(jobs, steps):
    in_specs, out_specs, out_shape = [], [], []
    for w, lead in jobs:
        rows, cols = w.shape[len(lead):]
        slab = rows // steps
        assert slab * steps == rows and slab % (2 * SUBLANES) == 0
        in_specs.append(pl.BlockSpec((None,) * len(lead) + (slab, cols),
                                     lambda i, lead=tuple(lead): lead + (i, 0)))
        out_specs.append(pl.BlockSpec((slab, cols), lambda i: (i, 0)))
        out_shape.append(jax.ShapeDtypeStruct((rows, cols), BF16))
    return in_specs, out_specs, out_shape


def _round_slabs(src_refs, dst_refs):
    for src, dst in zip(src_refs, dst_refs, strict=True):
        dst[...] = src[...].astype(BF16)


def _params(sem):
    return pltpu.CompilerParams(dimension_semantics=sem, vmem_limit_bytes=V7X_VMEM_LIMIT_BYTES)


def _ada_kernel(c_ref, w_ref, b_ref, o_ref):
    cond = _silu(c_ref[...]).astype(BF16)
    o_ref[...] = jnp.dot(cond, w_ref[...].astype(BF16), preferred_element_type=F32) + b_ref[...]


def _ada_call(c, w_ada, b_ada):
    n = N_SUB * 3 * D_MODEL
    return pl.pallas_call(
        _ada_kernel,
        grid=(DEPTH, n // ADA_TN),
        in_specs=[
            pl.BlockSpec((BATCH, D_MODEL), lambda l, j: (0, 0)),
            pl.BlockSpec((None, D_MODEL, ADA_TN), lambda l, j: (l, 0, j)),
            pl.BlockSpec((None, 1, ADA_TN), lambda l, j: (l, 0, j)),
        ],
        out_specs=pl.BlockSpec((None, BATCH, ADA_TN), lambda l, j: (l, 0, j)),
        out_shape=jax.ShapeDtypeStruct((DEPTH, BATCH, n), F32),
        compiler_params=_params(("arbitrary", "arbitrary")),
        name="ada_mod",
    )(c, w_ada, b_ada.reshape(DEPTH, 1, n))


def _rot_kernel(pos_ref, freq_ref, *rest, n_round):
    o_ref = rest[n_round]
    _round_slabs(rest[:n_round], rest[n_round + 1:])
    ang = freq_ref[...] * pos_ref[...].astype(F32)
    zeros = jnp.zeros((DIFF_HEAD_DIM - ROT_DIM, SEQ), F32)
    slab = jnp.concatenate([jnp.cos(ang), jnp.sin(ang), zeros], axis=0)
    feat = jnp.concatenate([slab, slab], axis=0)
    for t0 in range(0, SEQ, LANES):
        o_ref[t0:t0 + LANES, :] = feat[:, t0:t0 + LANES].T


def _rot_call(positions, round_jobs):
    inv_freq = ROPE_THETA ** (-jnp.arange(0, ROT_DIM, 2, dtype=F32) / ROT_DIM)
    slab_in, slab_out, slab_shape = _round_specs(round_jobs, BATCH)
    outs = pl.pallas_call(
        functools.partial(_rot_kernel, n_round=len(round_jobs)),
        grid=(BATCH,),
        in_specs=[
            pl.BlockSpec((None, 1, SEQ), lambda b: (b, 0, 0)),
            pl.BlockSpec((ROT_HALF, 1), lambda b: (0, 0)),
        ] + slab_in,
        out_specs=[pl.BlockSpec((SEQ, LANES), lambda b: (b, 0))] + slab_out,
        out_shape=[jax.ShapeDtypeStruct((TOKENS, LANES), F32)] + slab_shape,
        compiler_params=_params(("arbitrary",)),
        name="rotary_table",
    )(positions.reshape(BATCH, 1, SEQ), inv_freq.reshape(ROT_HALF, 1), *[w for w, _ in round_jobs])
    return outs[0], list(outs[1:])


def _mix_in_kernel(x_ref, shift_ref, scale_ref, g_ref, tab_ref, w_ref,
                   a_ref, q_ref, k_ref, vt_ref):
    dim = lax.broadcasted_iota(jnp.int32, (1, LANES), 1) % DIFF_HEAD_DIM
    lo, hi = dim < ROT_HALF, (dim >= ROT_HALF) & (dim < ROT_DIM)
    q_off = 2 * CONV_CH
    k_off = q_off + DIFF_WIDTH
    v_off = k_off + DIFF_WIDTH
    for t0 in range(0, TM, MIX_SUB):
        rows = slice(t0, t0 + MIX_SUB)
        h = _modulate(x_ref[rows, :], g_ref[...], shift_ref[...], scale_ref[...]).astype(BF16)
        z = jnp.dot(h, w_ref[...], preferred_element_type=F32)
        a_ref[rows, :] = z[:, :CONV_CH] * _sigmoid(z[:, CONV_CH:2 * CONV_CH])
        tab = tab_ref[rows, :]
        cmul = jnp.where(lo, tab, jnp.where(hi, pltpu.roll(tab, ROT_HALF, 1), 1.0))
        s_up = jnp.where(lo, -pltpu.roll(tab, LANES - ROT_HALF, 1), 0.0)
        s_dn = jnp.where(hi, tab, 0.0)
        for hd in range(DIFF_HEADS):
            for off, ref, mul in ((q_off, q_ref, Q_SCALE), (k_off, k_ref, None)):
                t = z[:, off + hd * LANES: off + (hd + 1) * LANES]
                r = (t * cmul + pltpu.roll(t, LANES - ROT_HALF, 1) * s_up
                     + pltpu.roll(t, ROT_HALF, 1) * s_dn)
                if mul is not None:
                    r = r * mul
                ref[rows, hd * LANES:(hd + 1) * LANES] = r.astype(BF16)
        for hd in range(DIFF_HEADS):
            r0 = hd * V_EXT
            vt_ref[r0:r0 + DIFF_V_DIM, rows] = (
                z[:, v_off + hd * LANES:v_off + (hd + 1) * LANES].T.astype(BF16))
            vt_ref[r0 + DIFF_V_DIM:r0 + V_EXT, rows] = jnp.ones((ATT_ONES_ROWS, MIX_SUB), BF16)


def _mix_in_call(x, mods, norm_g, tab, w_in, layer):
    tok = lambda w: pl.BlockSpec((TM, w), lambda i: (i, 0))
    return pl.pallas_call(
        _mix_in_kernel,
        grid=(N_TILES,),
        in_specs=[
            tok(D_MODEL),
            _mod_spec(layer, 1, 0),
            _mod_spec(layer, 1, 1),
            _pick_spec(norm_g.shape, layer, 1),
            tok(LANES),
            _pick_spec(w_in.shape),
        ],
        out_specs=[tok(CONV_CH), tok(DIFF_WIDTH), tok(DIFF_WIDTH),
                   pl.BlockSpec((None, DIFF_HEADS * V_EXT, TM),
                                lambda i: (i // TILES_PER_SEQ, 0, i % TILES_PER_SEQ))],
        out_shape=[
            jax.ShapeDtypeStruct((TOKENS, CONV_CH), F32),
            jax.ShapeDtypeStruct((TOKENS, DIFF_WIDTH), BF16),
            jax.ShapeDtypeStruct((TOKENS, DIFF_WIDTH), BF16),
            jax.ShapeDtypeStruct((BATCH, DIFF_HEADS * V_EXT, SEQ), BF16),
        ],
        compiler_params=_params(("arbitrary",)),
        name=f"mix_in_l{layer}",
    )(x, mods, mods, norm_g, tab, w_in)


def _attn_kernel(q_ref, k_ref, vt_ext, dl_ref, sg_ref, o_ref, *, lambda_init):
    dl = dl_ref[...]
    lam = (jnp.exp(jnp.sum(dl[0:1] * dl[1:2], axis=1, keepdims=True))
           - jnp.exp(jnp.sum(dl[2:3] * dl[3:4], axis=1, keepdims=True)) + lambda_init)
    first = lax.broadcasted_iota(jnp.int32, (1, LANES), 1) < DIFF_HEAD_DIM
    out_gain = sg_ref[...] * (1.0 - lambda_init)
    nt = (((1,), (1,)), ((), ()))
    visible = ((lax.broadcasted_iota(jnp.int32, (ATT_QBLK, 1), 0) // CHUNK)
               <= (lax.broadcasted_iota(jnp.int32, (1, 2 * ATT_QBLK), 1) % ATT_QBLK // CHUNK))

    def scores(item):
        hd, blk = item
        row0 = blk * ATT_QBLK
        klen = row0 + ATT_QBLK
        lanes = slice(hd * LANES, (hd + 1) * LANES)
        q = q_ref[row0:klen, lanes]
        q_both = jnp.concatenate([jnp.where(first, q, jnp.zeros_like(q)),
                                  jnp.where(first, jnp.zeros_like(q), q)], axis=0)
        return lax.dot_general(k_ref[0:klen, lanes], q_both, nt,
                               preferred_element_type=F32)

    items = [(hd, blk) for hd in range(ATT_HEADS_PER_STEP) for blk in range(SEQ // ATT_QBLK)]
    ahead = [scores(item) for item in items[:ATT_AHEAD]]
    for n, (hd, blk) in enumerate(items):
        row0 = blk * ATT_QBLK
        klen = row0 + ATT_QBLK
        s = ahead.pop(0)
        if n + ATT_AHEAD < len(items):
            ahead.append(scores(items[n + ATT_AHEAD]))
        s_diag = jnp.where(visible, s[row0:, :], -jnp.inf)
        m = jnp.max(s_diag, axis=0, keepdims=True)
        if row0:
            s_full = s[:row0, :]
            m = jnp.maximum(m, jnp.max(s_full, axis=0, keepdims=True))
        p = jnp.exp2(s_diag - m).astype(BF16)
        if row0:
            p = jnp.concatenate([jnp.exp2(s_full - m).astype(BF16), p], axis=0)
        pv = jnp.dot(vt_ext[hd * V_EXT:(hd + 1) * V_EXT, 0:klen], p,
                     preferred_element_type=F32)
        inv = 1.0 / pv[DIFF_V_DIM:DIFF_V_DIM + 1, :]
        o = (pv[:DIFF_V_DIM, :ATT_QBLK] * inv[:, :ATT_QBLK]
             - pv[:DIFF_V_DIM, ATT_QBLK:] * (lam * inv[:, ATT_QBLK:]))
        ms = jnp.mean(o * o, axis=0, keepdims=True)
        o_ref[row0:klen, hd * LANES:(hd + 1) * LANES] = (
            (o * lax.rsqrt(ms + EPS)).T * out_gain).astype(BF16)


def _attn_call(q, k, vt, diff_lambda, subln_g, layer):
    lambda_init = 0.8 - 0.6 * float(np.exp(-0.3 * layer))
    head = lambda: pl.BlockSpec((None, SEQ, ATT_HEADS_PER_STEP * LANES), lambda b, h: (b, 0, h))
    shp = (BATCH, SEQ, DIFF_WIDTH)
    return pl.pallas_call(
        functools.partial(_attn_kernel, lambda_init=lambda_init),
        grid=(BATCH, DIFF_HEADS // ATT_HEADS_PER_STEP),
        in_specs=[
            head(), head(),
            pl.BlockSpec((None, ATT_HEADS_PER_STEP * V_EXT, SEQ), lambda b, h: (b, h, 0)),
            _pick_spec(diff_lambda.shape, layer // 2),
            _pick_spec(subln_g.shape, layer // 2),
        ],
        out_specs=head(),
        out_shape=jax.ShapeDtypeStruct(shp, BF16),
        compiler_params=_params(("arbitrary", "arbitrary")),
        name=f"diff_attn_l{layer}",
    )(q.reshape(shp), k.reshape(shp), vt, diff_lambda, subln_g)


def _conv_out_stage(x_ref, gate_ref, a_ref, halo_ref, o_ref, cw_ref, cb_ref, lg_ref, lb_ref, w_ref,
                    abuf, shifted, conv, wrep):
    seq_start = (pl.program_id(0) % TILES_PER_SEQ) == 0
    halo = halo_ref[...]
    abuf[0:HALO, :] = jnp.where(seq_start, jnp.zeros_like(halo), halo)
    abuf[HALO:, :] = a_ref[...]
    lead = HALO - (CONV_WIDTH - 1)
    span = CONV_WIN + HALO - SUBLANES
    for j in range(CONV_WIDTH):
        wrep[j] = jnp.broadcast_to(cw_ref[j:j + 1, :], (SUBLANES, CONV_CH))
    bias = jnp.broadcast_to(cb_ref[...], (SUBLANES, CONV_CH))
    groups = CONV_ROWS // SUBLANES
    for base in range(0, TM, CONV_WIN):
        for r in range(1, SUBLANES):
            shifted[r - 1, 0:span, :] = abuf[base + r:base + r + span, :]
        for r0 in range(0, CONV_WIN, CONV_ROWS):
            accs = [bias] * groups
            for j in range(CONV_WIDTH):
                whole, r = divmod(lead + j, SUBLANES)
                w = wrep[j]
                for grp in range(groups):
                    start = r0 + (whole + grp) * SUBLANES
                    if r == 0:
                        src = abuf[base + start:base + start + SUBLANES, :]
                    else:
                        src = shifted[r - 1, start:start + SUBLANES, :]
                    accs[grp] = accs[grp] + w * src
            for grp in range(groups):
                row = base + r0 + grp * SUBLANES
                conv[row:row + SUBLANES, :] = accs[grp]
    c = conv[...]
    mu = jnp.mean(c, axis=-1, keepdims=True)
    d = c - mu
    var = jnp.mean(d * d, axis=-1, keepdims=True)
    a = _silu(d * lax.rsqrt(var + EPS) * lg_ref[...] + lb_ref[...]).astype(BF16)
    y = (jnp.dot(a, w_ref[0:CONV_CH, :], preferred_element_type=F32)
         + jnp.dot(o_ref[...], w_ref[CONV_CH:, :], preferred_element_type=F32))
    return x_ref[...] + gate_ref[...] * y


def _pool_stage(x_ref, xh_ref, shift_ref, scale_ref, gate_ref, g_ref, pw_ref, ps_ref, hbuf, level):
    t_in_seq = (pl.program_id(0) % TILES_PER_SEQ) * TM
    x = x_ref[...]
    g, shift, scale = g_ref[...], shift_ref[...], scale_ref[...]
    hh = _modulate(xh_ref[...], g, shift, scale)
    hbuf[0:POOL_HALO, :] = jnp.where(t_in_seq == 0, jnp.zeros_like(hh), hh)
    hbuf[POOL_HALO:, :] = _modulate(x, g, shift, scale)
    pos = (t_in_seq + 1 + lax.broadcasted_iota(jnp.int32, (TM, 1), 0)).astype(F32)
    rows = POOL_HALO + TM
    ys = []
    for grp, win in enumerate(POOL_WINDOWS):
        lo, hi = grp * POOL_GROUP_CH, (grp + 1) * POOL_GROUP_CH
        cur = hbuf[POOL_HALO:, lo:hi]
        src = lambda r0, r1: hbuf[r0:r1, lo:hi]
        lag, k = 1, 0
        while 2 * lag < win:
            start = SUBLANES * (k + 1)
            level[k, start:rows, :] = src(start, rows) + src(start - lag, rows - lag)
            src = lambda r0, r1, k=k: level[k, r0:r1, :]
            lag, k = 2 * lag, k + 1
        tot = src(POOL_HALO, rows) + src(POOL_HALO - lag, rows - lag)
        p = (tot / jnp.minimum(pos, float(win)) - cur).astype(BF16)
        ys.append(jnp.dot(p, pw_ref[grp], preferred_element_type=F32))
    y = jnp.concatenate(ys, axis=-1) * ps_ref[...]
    return x + gate_ref[...] * y


def _mixer_kernel(*refs, stage, n_in):
    out_ref = refs[n_in]
    out_ref[...] = stage(*refs[:n_in], *refs[n_in + 1:])


def _tok_spec(width):
    return pl.BlockSpec((TM, width), lambda i: (i, 0))


def _halo_spec(rows, width):
    return pl.BlockSpec((rows, width), lambda i: (jnp.maximum(i * (TM // rows) - 1, 0), 0))


def _mixer_call(stage, args, specs, scratch, name):
    return pl.pallas_call(
        functools.partial(_mixer_kernel, stage=stage, n_in=len(args)),
        grid=(N_TILES,),
        in_specs=specs,
        out_specs=_tok_spec(D_MODEL),
        out_shape=jax.ShapeDtypeStruct((TOKENS, D_MODEL), F32),
        scratch_shapes=scratch,
        compiler_params=_params(("arbitrary",)),
        name=name,
    )(*args)


def _conv_out_call(x, mods, a, o, conv_w, conv_b, ln_g, ln_b, w_out, layer):
    e = layer // 2
    specs = [
        _tok_spec(D_MODEL),
        _mod_spec(layer, 1, 2),
        _tok_spec(CONV_CH),
        _halo_spec(HALO, CONV_CH),
        _tok_spec(DIFF_WIDTH),
        _pick_spec(conv_w.shape, e),
        _pick_spec(conv_b.shape, e),
        _pick_spec(ln_g.shape, e),
        _pick_spec(ln_b.shape, e),
        _pick_spec(w_out.shape),
    ]
    scratch = [pltpu.VMEM((HALO + TM, CONV_CH), F32),
               pltpu.VMEM((SUBLANES - 1, CONV_WIN + HALO - SUBLANES, CONV_CH), F32),
               pltpu.VMEM((TM, CONV_CH), F32),
               pltpu.VMEM((CONV_WIDTH, SUBLANES, CONV_CH), F32)]
    return _mixer_call(_conv_out_stage, [x, mods, a, a, o, conv_w, conv_b, ln_g, ln_b, w_out],
                       specs, scratch, f"mix_out_l{layer}")


def _pool_call(x, mods, norm_g, pool_w, pool_scale, layer):
    specs = [
        _tok_spec(D_MODEL),
        _halo_spec(POOL_HALO, D_MODEL),
        _mod_spec(layer, 1, 0),
        _mod_spec(layer, 1, 1),
        _mod_spec(layer, 1, 2),
        _pick_spec(norm_g.shape, layer, 1),
        _pick_spec(pool_w.shape),
        _pick_spec(pool_scale.shape, layer // 2),
    ]
    n_levels = int(np.log2(max(POOL_WINDOWS))) - 1
    assert SUBLANES * (n_levels + 1) <= POOL_HALO and max(POOL_WINDOWS) <= POOL_HALO
    scratch = [pltpu.VMEM((POOL_HALO + TM, D_MODEL), F32),
               pltpu.VMEM((n_levels, POOL_HALO + TM, POOL_GROUP_CH), F32)]
    return _mixer_call(_pool_stage, [x, x, mods, mods, mods, norm_g, pool_w, pool_scale],
                       specs, scratch, f"pool_l{layer}")


def _ffn_kernel(x_ref, shift_ref, scale_ref, gate_ref, g_ref, w1_ref, w3_ref, w2_ref, fg_ref,
                *rest, final, n_round):
    o_ref = rest[n_round]
    _round_slabs(rest[:n_round], rest[n_round + 1:])

    for r0 in range(0, TM_FFN, FFN_SUB):
        x = x_ref[r0:r0 + FFN_SUB, :]
        h = _modulate(x, g_ref[...], shift_ref[...], scale_ref[...]).astype(BF16)
        a = jnp.dot(h, w1_ref[...], preferred_element_type=F32)
        b = jnp.dot(h, w3_ref[...], preferred_element_type=F32)
        u = (_silu(a) * b).astype(BF16)
        y = jnp.dot(u, w2_ref[...], preferred_element_type=F32)
        xo = x + (0.5 * gate_ref[...]) * y
        if final:
            ms = jnp.mean(xo * xo, axis=-1, keepdims=True)
            xo = xo * lax.rsqrt(ms + EPS) * fg_ref[...]
        o_ref[r0:r0 + FFN_SUB, :] = xo


def _ffn_call(x, mods, norm_g, weights, final_g, layer, sub, final, round_jobs):
    steps = TOKENS // TM_FFN
    slab_in, slab_out, slab_shape = _round_specs(round_jobs, steps)
    in_specs = [
        pl.BlockSpec((TM_FFN, D_MODEL), lambda i: (i, 0)),
        _mod_spec(layer, sub, 0, TM_FFN),
        _mod_spec(layer, sub, 1, TM_FFN),
        _mod_spec(layer, sub, 2, TM_FFN),
        _pick_spec(norm_g.shape, layer, sub),
        _pick_spec(weights[0].shape),
        _pick_spec(weights[1].shape),
        _pick_spec(weights[2].shape),
        _pick_spec(final_g.shape),
    ]
    args = [x, mods, mods, mods, norm_g, *weights, final_g] + [w for w, _ in round_jobs]
    outs = pl.pallas_call(
        functools.partial(_ffn_kernel, final=final, n_round=len(round_jobs)),
        grid=(steps,),
        in_specs=in_specs + slab_in,
        out_specs=[pl.BlockSpec((TM_FFN, D_MODEL), lambda i: (i, 0))] + slab_out,
        out_shape=[jax.ShapeDtypeStruct((TOKENS, D_MODEL), F32)] + slab_shape,
        compiler_params=_params(("arbitrary",)),
        name=f"ffn_l{layer}_s{sub}",
    )(*args)
    return outs[0], list(outs[1:])


def kernel(x, c, positions, w_ada, b_ada, norm_g, ffn_w1, ffn_w3, ffn_w2, w_in, w_out, conv_w, conv_b,
           conv_ln_g, conv_ln_b, diff_lambda, subln_g, pool_w, pool_scale, final_g):
    mods = _ada_call(c, w_ada, b_ada).reshape(DEPTH * MOD_ROWS_PER_LAYER, 1, D_MODEL)
    rows = lambda v: v.reshape(v.shape[:-1] + (1, v.shape[-1]))
    norm_g, final_g = rows(norm_g), rows(final_g)
    conv_b, conv_ln_g, conv_ln_b = rows(conv_b), rows(conv_ln_g), rows(conv_ln_b)
    subln_g, pool_scale = rows(subln_g), rows(pool_scale)
    pool_w2d = pool_w.reshape(pool_w.shape[0], len(POOL_WINDOWS) * POOL_GROUP_CH, POOL_GROUP_CH)

    ffn_jobs = lambda l, half: [(ffn_w1, (l, half)), (ffn_w3, (l, half)), (ffn_w2, (l, half))]
    tab, weights = _rot_call(positions, ffn_jobs(0, 0))

    xt = x.reshape(TOKENS, D_MODEL)
    for l in range(DEPTH):
        last = l == DEPTH - 1
        mixer_jobs = ([(w_in, (l // 2,)), (w_out, (l // 2,))] if l % 2 == 0
                      else [(pool_w2d, (l // 2,))])
        xt, rounded = _ffn_call(xt, mods, norm_g, weights, final_g, l, 0, False,
                                ffn_jobs(l, 1) + mixer_jobs)
        weights, mixer_w = rounded[:3], rounded[3:]
        if l % 2 == 0:
            w_in_b, w_out_b = mixer_w
            a, q, k, vt = _mix_in_call(xt, mods, norm_g, tab, w_in_b, l)
            o = _attn_call(q, k, vt, diff_lambda, subln_g, l)
            xt = _conv_out_call(xt, mods, a, o.reshape(TOKENS, DIFF_WIDTH), conv_w, conv_b,
                                conv_ln_g, conv_ln_b, w_out_b, l)
        else:
            pool_w_b = mixer_w[0].reshape(pool_w.shape[1:])
            xt = _pool_call(xt, mods, norm_g, pool_w_b, pool_scale, l)
        xt, weights = _ffn_call(xt, mods, norm_g, weights, final_g, l, 2, last,
                                [] if last else ffn_jobs(l + 1, 0))
    return xt.reshape(BATCH, SEQ, D_MODEL)
```

```python
import functools

import numpy as np
import jax
import jax.numpy as jnp
from jax import lax
from jax.experimental import pallas as pl
from jax.experimental.pallas import tpu as pltpu

D_MODEL = 1024
BATCH = 8
SEQ = 2048
DEPTH = 4
CHUNK = 64
CONV_CH = D_MODEL // 2
CONV_WIDTH = 31
DIFF_HEADS = 4
DIFF_HEAD_DIM = 64
DIFF_V_DIM = 2 * DIFF_HEAD_DIM
DIFF_WIDTH = DIFF_HEADS * DIFF_V_DIM
IN_COLS = 2 * CONV_CH + 3 * DIFF_WIDTH
MIX_WIDTH = CONV_CH + DIFF_WIDTH
ROPE_THETA = 500000.0
ROT_DIM = DIFF_HEAD_DIM // 4
ROT_HALF = ROT_DIM // 2
POOL_WINDOWS = (2, 4, 8, 16)
POOL_GROUP_CH = D_MODEL // len(POOL_WINDOWS)
D_FF = 2816
N_SUB = 3
EPS = 1e-6

TOKENS = BATCH * SEQ
MOD_ROWS_PER_BATCH = N_SUB * 3
MOD_ROWS_PER_LAYER = BATCH * MOD_ROWS_PER_BATCH

V7X_VMEM_LIMIT_BYTES = 56 * 1024 * 1024
LANES = 128
SUBLANES = 8
HALO = 32
POOL_HALO = 32
CONV_WIN = 512
CONV_ROWS = 32

TM = 1024
MIX_SUB = 256
TM_FFN = 1024
FFN_SUB = 256
N_TILES = TOKENS // TM
TILES_PER_SEQ = SEQ // TM
ADA_TN = 1536
ATT_QBLK = 256
ATT_ONES_ROWS = 16
V_EXT = DIFF_V_DIM + ATT_ONES_ROWS
ATT_AHEAD = 2
ATT_HEADS_PER_STEP = 4
Q_SCALE = DIFF_HEAD_DIM ** -0.5 * float(np.log2(np.e))

F32 = jnp.float32
BF16 = jnp.bfloat16


def _silu(v):
    return v * (1.0 / (1.0 + jnp.exp(-v)))


def _sigmoid(v):
    return 1.0 / (1.0 + jnp.exp(-v))


def _modulate(x, g, shift, scale):
    ms = jnp.mean(x * x, axis=-1, keepdims=True)
    return (x * lax.rsqrt(ms + EPS) * g) * (1.0 + scale) + shift


def _mod_spec(layer, sub, which, tile=TM):
    base = layer * MOD_ROWS_PER_LAYER + sub * 3 + which
    return pl.BlockSpec(
        (None, 1, D_MODEL),
        lambda i: (base + (i // (SEQ // tile)) * MOD_ROWS_PER_BATCH, 0, 0))


def _pick_spec(shape, *lead):
    tail = tuple(shape[len(lead):])
    return pl.BlockSpec((None,) * len(lead) + tail, lambda *_: tuple(lead) + (0,) * len(tail),
                        pipeline_mode=pl.Buffered(1))


def _round_specs(jobs, steps):
    in_specs, out_specs, out_shape = [], [], []
    for w, lead in jobs:
        rows, cols = w.shape[len(lead):]
        slab = rows // steps
        assert slab * steps == rows and slab % (2 * SUBLANES) == 0
        in_specs.append(pl.BlockSpec((None,) * len(lead) + (slab, cols),
                                     lambda i, lead=tuple(lead): lead + (i, 0)))
        out_specs.append(pl.BlockSpec((slab, cols), lambda i: (i, 0)))
        out_shape.append(jax.ShapeDtypeStruct((rows, cols), BF16))
    return in_specs, out_specs, out_shape


def _round_slabs(src_refs, dst_refs):
    for src, dst in zip(src_refs, dst_refs, strict=True):
        dst[...] = src[...].astype(BF16)


def _params(sem):
    return pltpu.CompilerParams(dimension_semantics=sem, vmem_limit_bytes=V7X_VMEM_LIMIT_BYTES)


def _ada_kernel(c_ref, w_ref, b_ref, o_ref):
    cond = _silu(c_ref[...]).astype(BF16)
    o_ref[...] = jnp.dot(cond, w_ref[...].astype(BF16), preferred_element_type=F32) + b_ref[...]


def _ada_call(c, w_ada, b_ada):
    n = N_SUB * 3 * D_MODEL
    return pl.pallas_call(
        _ada_kernel,
        grid=(DEPTH, n // ADA_TN),
        in_specs=[
            pl.BlockSpec((BATCH, D_MODEL), lambda l, j: (0, 0)),
            pl.BlockSpec((None, D_MODEL, ADA_TN), lambda l, j: (l, 0, j)),
            pl.BlockSpec((None, 1, ADA_TN), lambda l, j: (l, 0, j)),
        ],
        out_specs=pl.BlockSpec((None, BATCH, ADA_TN), lambda l, j: (l, 0, j)),
        out_shape=jax.ShapeDtypeStruct((DEPTH, BATCH, n), F32),
        compiler_params=_params(("arbitrary", "arbitrary")),
        name="ada_mod",
    )(c, w_ada, b_ada.reshape(DEPTH, 1, n))


def _rot_kernel(pos_ref, freq_ref, *rest, n_round):
    o_ref = rest[n_round]
    _round_slabs(rest[:n_round], rest[n_round + 1:])
    ang = freq_ref[...] * pos_ref[...].astype(F32)
    zeros = jnp.zeros((DIFF_HEAD_DIM - ROT_DIM, SEQ), F32)
    slab = jnp.concatenate([jnp.cos(ang), jnp.sin(ang), zeros], axis=0)
    feat = jnp.concatenate([slab, slab], axis=0)
    for t0 in range(0, SEQ, LANES):
        o_ref[t0:t0 + LANES, :] = feat[:, t0:t0 + LANES].T


def _rot_call(positions, round_jobs):
    inv_freq = ROPE_THETA ** (-jnp.arange(0, ROT_DIM, 2, dtype=F32) / ROT_DIM)
    slab_in, slab_out, slab_shape = _round_specs(round_jobs, BATCH)
    outs = pl.pallas_call(
        functools.partial(_rot_kernel, n_round=len(round_jobs)),
        grid=(BATCH,),
        in_specs=[
            pl.BlockSpec((None, 1, SEQ), lambda b: (b, 0, 0)),
            pl.BlockSpec((ROT_HALF, 1), lambda b: (0, 0)),
        ] + slab_in,
        out_specs=[pl.BlockSpec((SEQ, LANES), lambda b: (b, 0))] + slab_out,
        out_shape=[jax.ShapeDtypeStruct((TOKENS, LANES), F32)] + slab_shape,
        compiler_params=_params(("arbitrary",)),
        name="rotary_table",
    )(positions.reshape(BATCH, 1, SEQ), inv_freq.reshape(ROT_HALF, 1), *[w for w, _ in round_jobs])
    return outs[0], list(outs[1:])


def _mix_in_kernel(x_ref, shift_ref, scale_ref, g_ref, tab_ref, w_ref,
                   a_ref, q_ref, k_ref, vt_ref):
    dim = lax.broadcasted_iota(jnp.int32, (1, LANES), 1) % DIFF_HEAD_DIM
    lo, hi = dim < ROT_HALF, (dim >= ROT_HALF) & (dim < ROT_DIM)
    q_off = 2 * CONV_CH
    k_off = q_off + DIFF_WIDTH
    v_off = k_off + DIFF_WIDTH
    for t0 in range(0, TM, MIX_SUB):
        rows = slice(t0, t0 + MIX_SUB)
        h = _modulate(x_ref[rows, :], g_ref[...], shift_ref[...], scale_ref[...]).astype(BF16)
        z = jnp.dot(h, w_ref[...], preferred_element_type=F32)
        a_ref[rows, :] = z[:, :CONV_CH] * _sigmoid(z[:, CONV_CH:2 * CONV_CH])
        tab = tab_ref[rows, :]
        cmul = jnp.where(lo, tab, jnp.where(hi, pltpu.roll(tab, ROT_HALF, 1), 1.0))
        s_up = jnp.where(lo, -pltpu.roll(tab, LANES - ROT_HALF, 1), 0.0)
        s_dn = jnp.where(hi, tab, 0.0)
        for hd in range(DIFF_HEADS):
            for off, ref, mul in ((q_off, q_ref, Q_SCALE), (k_off, k_ref, None)):
                t = z[:, off + hd * LANES: off + (hd + 1) * LANES]
                r = (t * cmul + pltpu.roll(t, LANES - ROT_HALF, 1) * s_up
                     + pltpu.roll(t, ROT_HALF, 1) * s_dn)
                if mul is not None:
                    r = r * mul
                ref[rows, hd * LANES:(hd + 1) * LANES] = r.astype(BF16)
        for hd in range(DIFF_HEADS):
            r0 = hd * V_EXT
            vt_ref[r0:r0 + DIFF_V_DIM, rows] = (
                z[:, v_off + hd * LANES:v_off + (hd + 1) * LANES].T.astype(BF16))
            vt_ref[r0 + DIFF_V_DIM:r0 + V_EXT, rows] = jnp.ones((ATT_ONES_ROWS, MIX_SUB), BF16)


def _mix_in_call(x, mods, norm_g, tab, w_in, layer):
    tok = lambda w: pl.BlockSpec((TM, w), lambda i: (i, 0))
    return pl.pallas_call(
        _mix_in_kernel,
        grid=(N_TILES,),
        in_specs=[
            tok(D_MODEL),
            _mod_spec(layer, 1, 0),
            _mod_spec(layer, 1, 1),
            _pick_spec(norm_g.shape, layer, 1),
            tok(LANES),
            _pick_spec(w_in.shape),
        ],
        out_specs=[tok(CONV_CH), tok(DIFF_WIDTH), tok(DIFF_WIDTH),
                   pl.BlockSpec((None, DIFF_HEADS * V_EXT, TM),
                                lambda i: (i // TILES_PER_SEQ, 0, i % TILES_PER_SEQ))],
        out_shape=[
            jax.ShapeDtypeStruct((TOKENS, CONV_CH), F32),
            jax.ShapeDtypeStruct((TOKENS, DIFF_WIDTH), BF16),
            jax.ShapeDtypeStruct((TOKENS, DIFF_WIDTH), BF16),
            jax.ShapeDtypeStruct((BATCH, DIFF_HEADS * V_EXT, SEQ), BF16),
        ],
        compiler_params=_params(("arbitrary",)),
        name=f"mix_in_l{layer}",
    )(x, mods, mods, norm_g, tab, w_in)


def _attn_kernel(q_ref, k_ref, vt_ext, dl_ref, sg_ref, o_ref, *, lambda_init):
    dl = dl_ref[...]
    lam = (jnp.exp(jnp.sum(dl[0:1] * dl[1:2], axis=1, keepdims=True))
           - jnp.exp(jnp.sum(dl[2:3] * dl[3:4], axis=1, keepdims=True)) + lambda_init)
    first = lax.broadcasted_iota(jnp.int32, (1, LANES), 1) < DIFF_HEAD_DIM
    out_gain = sg_ref[...] * (1.0 - lambda_init)
    nt = (((1,), (1,)), ((), ()))
    visible = ((lax.broadcasted_iota(jnp.int32, (ATT_QBLK, 1), 0) // CHUNK)
               <= (lax.broadcasted_iota(jnp.int32, (1, 2 * ATT_QBLK), 1) % ATT_QBLK // CHUNK))

    def scores(item):
        hd, blk = item
        row0 = blk * ATT_QBLK
        klen = row0 + ATT_QBLK
        lanes = slice(hd * LANES, (hd + 1) * LANES)
        q = q_ref[row0:klen, lanes]
        q_both = jnp.concatenate([jnp.where(first, q, jnp.zeros_like(q)),
                                  jnp.where(first, jnp.zeros_like(q), q)], axis=0)
        return lax.dot_general(k_ref[0:klen, lanes], q_both, nt,
                               preferred_element_type=F32)

    items = [(hd, blk) for hd in range(ATT_HEADS_PER_STEP) for blk in range(SEQ // ATT_QBLK)]
    ahead = [scores(item) for item in items[:ATT_AHEAD]]
    for n, (hd, blk) in enumerate(items):
        row0 = blk * ATT_QBLK
        klen = row0 + ATT_QBLK
        s = ahead.pop(0)
        if n + ATT_AHEAD < len(items):
            ahead.append(scores(items[n + ATT_AHEAD]))
        s_diag = jnp.where(visible, s[row0:, :], -jnp.inf)
        m = jnp.max(s_diag, axis=0, keepdims=True)
        if row0:
            s_full = s[:row0, :]
            m = jnp.maximum(m, jnp.max(s_full, axis=0, keepdims=True))
        p = jnp.exp2(s_diag - m).astype(BF16)
        if row0:
            p = jnp.concatenate([jnp.exp2(s_full - m).astype(BF16), p], axis=0)
        pv = jnp.dot(vt_ext[hd * V_EXT:(hd + 1) * V_EXT, 0:klen], p,
                     preferred_element_type=F32)
        inv = 1.0 / pv[DIFF_V_DIM:DIFF_V_DIM + 1, :]
        o = (pv[:DIFF_V_DIM, :ATT_QBLK] * inv[:, :ATT_QBLK]
             - pv[:DIFF_V_DIM, ATT_QBLK:] * (lam * inv[:, ATT_QBLK:]))
        ms = jnp.mean(o * o, axis=0, keepdims=True)
        o_ref[row0:klen, hd * LANES:(hd + 1) * LANES] = (
            (o * lax.rsqrt(ms + EPS)).T * out_gain).astype(BF16)


def _attn_call(q, k, vt, diff_lambda, subln_g, layer):
    lambda_init = 0.8 - 0.6 * float(np.exp(-0.3 * layer))
    head = lambda: pl.BlockSpec((None, SEQ, ATT_HEADS_PER_STEP * LANES), lambda b, h: (b, 0, h))
    shp = (BATCH, SEQ, DIFF_WIDTH)
    return pl.pallas_call(
        functools.partial(_attn_kernel, lambda_init=lambda_init),
        grid=(BATCH, DIFF_HEADS // ATT_HEADS_PER_STEP),
        in_specs=[
            head(), head(),
            pl.BlockSpec((None, ATT_HEADS_PER_STEP * V_EXT, SEQ), lambda b, h: (b, h, 0)),
            _pick_spec(diff_lambda.shape, layer // 2),
            _pick_spec(subln_g.shape, layer // 2),
        ],
        out_specs=head(),
        out_shape=jax.ShapeDtypeStruct(shp, BF16),
        compiler_params=_params(("arbitrary", "arbitrary")),
        name=f"diff_attn_l{layer}",
    )(q.reshape(shp), k.reshape(shp), vt, diff_lambda, subln_g)


def _conv_out_stage(x_ref, gate_ref, a_ref, halo_ref, o_ref, cw_ref, cb_ref, lg_ref, lb_ref, w_ref,
                    abuf, shifted, conv, wrep):
    seq_start = (pl.program_id(0) % TILES_PER_SEQ) == 0
    halo = halo_ref[...]
    abuf[0:HALO, :] = jnp.where(seq_start, jnp.zeros_like(halo), halo)
    abuf[HALO:, :] = a_ref[...]
    lead = HALO - (CONV_WIDTH - 1)
    span = CONV_WIN + HALO - SUBLANES
    for j in range(CONV_WIDTH):
        wrep[j] = jnp.broadcast_to(cw_ref[j:j + 1, :], (SUBLANES, CONV_CH))
    bias = jnp.broadcast_to(cb_ref[...], (SUBLANES, CONV_CH))
    groups = CONV_ROWS // SUBLANES
    for base in range(0, TM, CONV_WIN):
        for r in range(1, SUBLANES):
            shifted[r - 1, 0:span, :] = abuf[base + r:base + r + span, :]
        for r0 in range(0, CONV_WIN, CONV_ROWS):
            accs = [bias] * groups
            for j in range(CONV_WIDTH):
                whole, r = divmod(lead + j, SUBLANES)
                w = wrep[j]
                for grp in range(groups):
                    start = r0 + (whole + grp) * SUBLANES
                    if r == 0:
                        src = abuf[base + start:base + start + SUBLANES, :]
                    else:
                        src = shifted[r - 1, start:start + SUBLANES, :]
                    accs[grp] = accs[grp] + w * src
            for grp in range(groups):
                row = base + r0 + grp * SUBLANES
                conv[row:row + SUBLANES, :] = accs[grp]
    c = conv[...]
    mu = jnp.mean(c, axis=-1, keepdims=True)
    d = c - mu
    var = jnp.mean(d * d, axis=-1, keepdims=True)
    a = _silu(d * lax.rsqrt(var + EPS) * lg_ref[...] + lb_ref[...]).astype(BF16)
    y = (jnp.dot(a, w_ref[0:CONV_CH, :], preferred_element_type=F32)
         + jnp.dot(o_ref[...], w_ref[CONV_CH:, :], preferred_element_type=F32))
    return x_ref[...] + gate_ref[...] * y


def _pool_stage(x_ref, xh_ref, shift_ref, scale_ref, gate_ref, g_ref, pw_ref, ps_ref, hbuf, level):
    t_in_seq = (pl.program_id(0) % TILES_PER_SEQ) * TM
    x = x_ref[...]
    g, shift, scale = g_ref[...], shift_ref[...], scale_ref[...]
    hh = _modulate(xh_ref[...], g, shift, scale)
    hbuf[0:POOL_HALO, :] = jnp.where(t_in_seq == 0, jnp.zeros_like(hh), hh)
    hbuf[POOL_HALO:, :] = _modulate(x, g, shift, scale)
    pos = (t_in_seq + 1 + lax.broadcasted_iota(jnp.int32, (TM, 1), 0)).astype(F32)
    rows = POOL_HALO + TM
    ys = []
    for grp, win in enumerate(POOL_WINDOWS):
        lo, hi = grp * POOL_GROUP_CH, (grp + 1) * POOL_GROUP_CH
        cur = hbuf[POOL_HALO:, lo:hi]
        src = lambda r0, r1: hbuf[r0:r1, lo:hi]
        lag, k = 1, 0
        while 2 * lag < win:
            start = SUBLANES * (k + 1)
            level[k, start:rows, :] = src(start, rows) + src(start - lag, rows - lag)
            src = lambda r0, r1, k=k: level[k, r0:r1, :]
            lag, k = 2 * lag, k + 1
        tot = src(POOL_HALO, rows) + src(POOL_HALO - lag, rows - lag)
        p = (tot / jnp.minimum(pos, float(win)) - cur).astype(BF16)
        ys.append(jnp.dot(p, pw_ref[grp], preferred_element_type=F32))
    y = jnp.concatenate(ys, axis=-1) * ps_ref[...]
    return x + gate_ref[...] * y


def _mixer_kernel(*refs, stage, n_in):
    out_ref = refs[n_in]
    out_ref[...] = stage(*refs[:n_in], *refs[n_in + 1:])


def _tok_spec(width):
    return pl.BlockSpec((TM, width), lambda i: (i, 0))


def _halo_spec(rows, width):
    return pl.BlockSpec((rows, width), lambda i: (jnp.maximum(i * (TM // rows) - 1, 0), 0))


def _mixer_call(stage, args, specs, scratch, name):
    return pl.pallas_call(
        functools.partial(_mixer_kernel, stage=stage, n_in=len(args)),
        grid=(N_TILES,),
        in_specs=specs,
        out_specs=_tok_spec(D_MODEL),
        out_shape=jax.ShapeDtypeStruct((TOKENS, D_MODEL), F32),
        scratch_shapes=scratch,
        compiler_params=_params(("arbitrary",)),
        name=name,
    )(*args)


def _conv_out_call(x, mods, a, o, conv_w, conv_b, ln_g, ln_b, w_out, layer):
    e = layer // 2
    specs = [
        _tok_spec(D_MODEL),
        _mod_spec(layer, 1, 2),
        _tok_spec(CONV_CH),
        _halo_spec(HALO, CONV_CH),
        _tok_spec(DIFF_WIDTH),
        _pick_spec(conv_w.shape, e),
        _pick_spec(conv_b.shape, e),
        _pick_spec(ln_g.shape, e),
        _pick_spec(ln_b.shape, e),
        _pick_spec(w_out.shape),
    ]
    scratch = [pltpu.VMEM((HALO + TM, CONV_CH), F32),
               pltpu.VMEM((SUBLANES - 1, CONV_WIN + HALO - SUBLANES, CONV_CH), F32),
               pltpu.VMEM((TM, CONV_CH), F32),
               pltpu.VMEM((CONV_WIDTH, SUBLANES, CONV_CH), F32)]
    return _mixer_call(_conv_out_stage, [x, mods, a, a, o, conv_w, conv_b, ln_g, ln_b, w_out],
                       specs, scratch, f"mix_out_l{layer}")


def _pool_call(x, mods, norm_g, pool_w, pool_scale, layer):
    specs = [
        _tok_spec(D_MODEL),
        _halo_spec(POOL_HALO, D_MODEL),
        _mod_spec(layer, 1, 0),
        _mod_spec(layer, 1, 1),
        _mod_spec(layer, 1, 2),
        _pick_spec(norm_g.shape, layer, 1),
        _pick_spec(pool_w.shape),
        _pick_spec(pool_scale.shape, layer // 2),
    ]
    n_levels = int(np.log2(max(POOL_WINDOWS))) - 1
    assert SUBLANES * (n_levels + 1) <= POOL_HALO and max(POOL_WINDOWS) <= POOL_HALO
    scratch = [pltpu.VMEM((POOL_HALO + TM, D_MODEL), F32),
               pltpu.VMEM((n_levels, POOL_HALO + TM, POOL_GROUP_CH), F32)]
    return _mixer_call(_pool_stage, [x, x, mods, mods, mods, norm_g, pool_w, pool_scale],
                       specs, scratch, f"pool_l{layer}")


def _ffn_kernel(x_ref, shift_ref, scale_ref, gate_ref, g_ref, w1_ref, w3_ref, w2_ref, fg_ref,
                *rest, final, n_round):
    o_ref = rest[n_round]
    _round_slabs(rest[:n_round], rest[n_round + 1:])

    for r0 in range(0, TM_FFN, FFN_SUB):
        x = x_ref[r0:r0 + FFN_SUB, :]
        h = _modulate(x, g_ref[...], shift_ref[...], scale_ref[...]).astype(BF16)
        a = jnp.dot(h, w1_ref[...], preferred_element_type=F32)
        b = jnp.dot(h, w3_ref[...], preferred_element_type=F32)
        u = (_silu(a) * b).astype(BF16)
        y = jnp.dot(u, w2_ref[...], preferred_element_type=F32)
        xo = x + (0.5 * gate_ref[...]) * y
        if final:
            ms = jnp.mean(xo * xo, axis=-1, keepdims=True)
            xo = xo * lax.rsqrt(ms + EPS) * fg_ref[...]
        o_ref[r0:r0 + FFN_SUB, :] = xo


def _ffn_call(x, mods, norm_g, weights, final_g, layer, sub, final, round_jobs):
    steps = TOKENS // TM_FFN
    slab_in, slab_out, slab_shape = _round_specs(round_jobs, steps)
    in_specs = [
        pl.BlockSpec((TM_FFN, D_MODEL), lambda i: (i, 0)),
        _mod_spec(layer, sub, 0, TM_FFN),
        _mod_spec(layer, sub, 1, TM_FFN),
        _mod_spec(layer, sub, 2, TM_FFN),
        _pick_spec(norm_g.shape, layer, sub),
        _pick_spec(weights[0].shape),
        _pick_spec(weights[1].shape),
        _pick_spec(weights[2].shape),
        _pick_spec(final_g.shape),
    ]
    args = [x, mods, mods, mods, norm_g, *weights, final_g] + [w for w, _ in round_jobs]
    outs = pl.pallas_call(
        functools.partial(_ffn_kernel, final=final, n_round=len(round_jobs)),
        grid=(steps,),
        in_specs=in_specs + slab_in,
        out_specs=[pl.BlockSpec((TM_FFN, D_MODEL), lambda i: (i, 0))] + slab_out,
        out_shape=[jax.ShapeDtypeStruct((TOKENS, D_MODEL), F32)] + slab_shape,
        compiler_params=_params(("arbitrary",)),
        name=f"ffn_l{layer}_s{sub}",
    )(*args)
    return outs[0], list(outs[1:])


def kernel(x, c, positions, w_ada, b_ada, norm_g, ffn_w1, ffn_w3, ffn_w2, w_in, w_out, conv_w, conv_b,
           conv_ln_g, conv_ln_b, diff_lambda, subln_g, pool_w, pool_scale, final_g):
    mods = _ada_call(c, w_ada, b_ada).reshape(DEPTH * MOD_ROWS_PER_LAYER, 1, D_MODEL)
    rows = lambda v: v.reshape(v.shape[:-1] + (1, v.shape[-1]))
    norm_g, final_g = rows(norm_g), rows(final_g)
    conv_b, conv_ln_g, conv_ln_b = rows(conv_b), rows(conv_ln_g), rows(conv_ln_b)
    subln_g, pool_scale = rows(subln_g), rows(pool_scale)
    pool_w2d = pool_w.reshape(pool_w.shape[0], len(POOL_WINDOWS) * POOL_GROUP_CH, POOL_GROUP_CH)

    ffn_jobs = lambda l, half: [(ffn_w1, (l, half)), (ffn_w3, (l, half)), (ffn_w2, (l, half))]
    tab, weights = _rot_call(positions, ffn_jobs(0, 0))

    xt = x.reshape(TOKENS, D_MODEL)
    for l in range(DEPTH):
        last = l == DEPTH - 1
        mixer_jobs = ([(w_in, (l // 2,)), (w_out, (l // 2,))] if l % 2 == 0
                      else [(pool_w2d, (l // 2,))])
        xt, rounded = _ffn_call(xt, mods, norm_g, weights, final_g, l, 0, False,
                                ffn_jobs(l, 1) + mixer_jobs)
        weights, mixer_w = rounded[:3], rounded[3:]
        if l % 2 == 0:
            w_in_b, w_out_b = mixer_w
            a, q, k, vt = _mix_in_call(xt, mods, norm_g, tab, w_in_b, l)
            o = _attn_call(q, k, vt, diff_lambda, subln_g, l)
            xt = _conv_out_call(xt, mods, a, o.reshape(TOKENS, DIFF_WIDTH), conv_w, conv_b,
                                conv_ln_g, conv_ln_b, w_out_b, l)
        else:
            pool_w_b = mixer_w[0].reshape(pool_w.shape[1:])
            xt = _pool_call(xt, mods, norm_g, pool_w_b, pool_scale, l)
        xt, weights = _ffn_call(xt, mods, norm_g, weights, final_g, l, 2, last,
                                [] if last else ffn_jobs(l + 1, 0))
    return xt.reshape(BATCH, SEQ, D_MODEL)
```

```python
import functools

import numpy as np
import jax
import jax.numpy as jnp
from jax import lax
from jax.experimental import pallas as pl
from jax.experimental.pallas import tpu as pltpu

D_MODEL = 1024
BATCH = 8
SEQ = 2048
DEPTH = 4
CHUNK = 64
CONV_CH = D_MODEL // 2
CONV_WIDTH = 31
DIFF_HEADS = 4
DIFF_HEAD_DIM = 64
DIFF_V_DIM = 2 * DIFF_HEAD_DIM
DIFF_WIDTH = DIFF_HEADS * DIFF_V_DIM
IN_COLS = 2 * CONV_CH + 3 * DIFF_WIDTH
MIX_WIDTH = CONV_CH + DIFF_WIDTH
ROPE_THETA = 500000.0
ROT_DIM = DIFF_HEAD_DIM // 4
ROT_HALF = ROT_DIM // 2
POOL_WINDOWS = (2, 4, 8, 16)
POOL_GROUP_CH = D_MODEL // len(POOL_WINDOWS)
D_FF = 2816
N_SUB = 3
EPS = 1e-6

TOKENS = BATCH * SEQ
MOD_ROWS_PER_BATCH = N_SUB * 3
MOD_ROWS_PER_LAYER = BATCH * MOD_ROWS_PER_BATCH

V7X_VMEM_LIMIT_BYTES = 56 * 1024 * 1024
LANES = 128
SUBLANES = 8
HALO = 32
POOL_HALO = 32
CONV_WIN = 512
CONV_ROWS = 32

TM = 1024
MIX_SUB = 256
TM_FFN = 1024
FFN_SUB = 256
FFN_SUB_AFTER_LIGHT = 128
N_TILES = TOKENS // TM
TILES_PER_SEQ = SEQ // TM
ADA_TN = 1536
ATT_QBLK = 256
ATT_ONES_ROWS = 16
V_EXT = DIFF_V_DIM + ATT_ONES_ROWS
ATT_AHEAD = 2
ATT_HEADS_PER_STEP = 4
Q_SCALE = DIFF_HEAD_DIM ** -0.5 * float(np.log2(np.e))

F32 = jnp.float32
BF16 = jnp.bfloat16


def _silu(v):
    return v * (1.0 / (1.0 + jnp.exp(-v)))


def _sigmoid(v):
    return 1.0 / (1.0 + jnp.exp(-v))


def _modulate(x, g, shift, scale):
    ms = jnp.mean(x * x, axis=-1, keepdims=True)
    return (x * lax.rsqrt(ms + EPS) * g) * (1.0 + scale) + shift


def _mod_spec(layer, sub, which, tile=TM):
    base = layer * MOD_ROWS_PER_LAYER + sub * 3 + which
    return pl.BlockSpec(
        (None, 1, D_MODEL),
        lambda i: (base + (i // (SEQ // tile)) * MOD_ROWS_PER_BATCH, 0, 0))


def _pick_spec(shape, *lead):
    tail = tuple(shape[len(lead):])
    return pl.BlockSpec((None,) * len(lead) + tail, lambda *_: tuple(lead) + (0,) * len(tail),
                        pipeline_mode=pl.Buffered(1))


def _round_specs(jobs, steps):
    in_specs, out_specs, out_shape = [], [], []
    for w, lead in jobs:
        rows, cols = w.shape[len(lead):]
        slab = rows // steps
        assert slab * steps == rows and slab % (2 * SUBLANES) == 0
        in_specs.append(pl.BlockSpec((None,) * len(lead) + (slab, cols),
                                     lambda i, lead=tuple(lead): lead + (i, 0)))
        out_specs.append(pl.BlockSpec((slab, cols), lambda i: (i, 0)))
        out_shape.append(jax.ShapeDtypeStruct((rows, cols), BF16))
    return in_specs, out_specs, out_shape


def _round_slabs(src_refs, dst_refs):
    for src, dst in zip(src_refs, dst_refs, strict=True):
        dst[...] = src[...].astype(BF16)


def _params(sem):
    return pltpu.CompilerParams(dimension_semantics=sem, vmem_limit_bytes=V7X_VMEM_LIMIT_BYTES)


def _ada_kernel(c_ref, w_ref, b_ref, o_ref):
    cond = _silu(c_ref[...]).astype(BF16)
    o_ref[...] = jnp.dot(cond, w_ref[...].astype(BF16), preferred_element_type=F32) + b_ref[...]


def _ada_call(c, w_ada, b_ada):
    n = N_SUB * 3 * D_MODEL
    return pl.pallas_call(
        _ada_kernel,
        grid=(DEPTH, n // ADA_TN),
        in_specs=[
            pl.BlockSpec((BATCH, D_MODEL), lambda l, j: (0, 0)),
            pl.BlockSpec((None, D_MODEL, ADA_TN), lambda l, j: (l, 0, j)),
            pl.BlockSpec((None, 1, ADA_TN), lambda l, j: (l, 0, j)),
        ],
        out_specs=pl.BlockSpec((None, BATCH, ADA_TN), lambda l, j: (l, 0, j)),
        out_shape=jax.ShapeDtypeStruct((DEPTH, BATCH, n), F32),
        compiler_params=_params(("arbitrary", "arbitrary")),
        name="ada_mod",
    )(c, w_ada, b_ada.reshape(DEPTH, 1, n))


def _rot_kernel(pos_ref, freq_ref, *rest, n_round):
    o_ref = rest[n_round]
    _round_slabs(rest[:n_round], rest[n_round + 1:])
    ang = freq_ref[...] * pos_ref[...].astype(F32)
    zeros = jnp.zeros((DIFF_HEAD_DIM - ROT_DIM, SEQ), F32)
    slab = jnp.concatenate([jnp.cos(ang), jnp.sin(ang), zeros], axis=0)
    feat = jnp.concatenate([slab, slab], axis=0)
    for t0 in range(0, SEQ, LANES):
        o_ref[t0:t0 + LANES, :] = feat[:, t0:t0 + LANES].T


def _rot_call(positions, round_jobs):
    inv_freq = ROPE_THETA ** (-jnp.arange(0, ROT_DIM, 2, dtype=F32) / ROT_DIM)
    slab_in, slab_out, slab_shape = _round_specs(round_jobs, BATCH)
    outs = pl.pallas_call(
        functools.partial(_rot_kernel, n_round=len(round_jobs)),
        grid=(BATCH,),
        in_specs=[
            pl.BlockSpec((None, 1, SEQ), lambda b: (b, 0, 0)),
            pl.BlockSpec((ROT_HALF, 1), lambda b: (0, 0)),
        ] + slab_in,
        out_specs=[pl.BlockSpec((SEQ, LANES), lambda b: (b, 0))] + slab_out,
        out_shape=[jax.ShapeDtypeStruct((TOKENS, LANES), F32)] + slab_shape,
        compiler_params=_params(("arbitrary",)),
        name="rotary_table",
    )(positions.reshape(BATCH, 1, SEQ), inv_freq.reshape(ROT_HALF, 1), *[w for w, _ in round_jobs])
    return outs[0], list(outs[1:])


def _mix_in_kernel(x_ref, shift_ref, scale_ref, g_ref, tab_ref, w_ref,
                   a_ref, q_ref, k_ref, vt_ref):
    dim = lax.broadcasted_iota(jnp.int32, (1, LANES), 1) % DIFF_HEAD_DIM
    lo, hi = dim < ROT_HALF, (dim >= ROT_HALF) & (dim < ROT_DIM)
    q_off = 2 * CONV_CH
    k_off = q_off + DIFF_WIDTH
    v_off = k_off + DIFF_WIDTH
    for t0 in range(0, TM, MIX_SUB):
        rows = slice(t0, t0 + MIX_SUB)
        h = _modulate(x_ref[rows, :], g_ref[...], shift_ref[...], scale_ref[...]).astype(BF16)
        z = jnp.dot(h, w_ref[...], preferred_element_type=F32)
        a_ref[rows, :] = z[:, :CONV_CH] * _sigmoid(z[:, CONV_CH:2 * CONV_CH])
        tab = tab_ref[rows, :]
        cmul = jnp.where(lo, tab, jnp.where(hi, pltpu.roll(tab, ROT_HALF, 1), 1.0))
        s_up = jnp.where(lo, -pltpu.roll(tab, LANES - ROT_HALF, 1), 0.0)
        s_dn = jnp.where(hi, tab, 0.0)
        for hd in range(DIFF_HEADS):
            for off, ref, mul in ((q_off, q_ref, Q_SCALE), (k_off, k_ref, None)):
                t = z[:, off + hd * LANES: off + (hd + 1) * LANES]
                r = (t * cmul + pltpu.roll(t, LANES - ROT_HALF, 1) * s_up
                     + pltpu.roll(t, ROT_HALF, 1) * s_dn)
                if mul is not None:
                    r = r * mul
                ref[rows, hd * LANES:(hd + 1) * LANES] = r.astype(BF16)
        for hd in range(DIFF_HEADS):
            r0 = hd * V_EXT
            vt_ref[r0:r0 + DIFF_V_DIM, rows] = (
                z[:, v_off + hd * LANES:v_off + (hd + 1) * LANES].T.astype(BF16))
            vt_ref[r0 + DIFF_V_DIM:r0 + V_EXT, rows] = jnp.ones((ATT_ONES_ROWS, MIX_SUB), BF16)


def _mix_in_call(x, mods, norm_g, tab, w_in, layer):
    tok = lambda w: pl.BlockSpec((TM, w), lambda i: (i, 0))
    return pl.pallas_call(
        _mix_in_kernel,
        grid=(N_TILES,),
        in_specs=[
            tok(D_MODEL),
            _mod_spec(layer, 1, 0),
            _mod_spec(layer, 1, 1),
            _pick_spec(norm_g.shape, layer, 1),
            tok(LANES),
            _pick_spec(w_in.shape),
        ],
        out_specs=[tok(CONV_CH), tok(DIFF_WIDTH), tok(DIFF_WIDTH),
                   pl.BlockSpec((None, DIFF_HEADS * V_EXT, TM),
                                lambda i: (i // TILES_PER_SEQ, 0, i % TILES_PER_SEQ))],
        out_shape=[
            jax.ShapeDtypeStruct((TOKENS, CONV_CH), F32),
            jax.ShapeDtypeStruct((TOKENS, DIFF_WIDTH), BF16),
            jax.ShapeDtypeStruct((TOKENS, DIFF_WIDTH), BF16),
            jax.ShapeDtypeStruct((BATCH, DIFF_HEADS * V_EXT, SEQ), BF16),
        ],
        compiler_params=_params(("arbitrary",)),
        name=f"mix_in_l{layer}",
    )(x, mods, mods, norm_g, tab, w_in)


def _attn_kernel(q_ref, k_ref, vt_ext, dl_ref, sg_ref, o_ref, *, lambda_init):
    dl = dl_ref[...]
    lam = (jnp.exp(jnp.sum(dl[0:1] * dl[1:2], axis=1, keepdims=True))
           - jnp.exp(jnp.sum(dl[2:3] * dl[3:4], axis=1, keepdims=True)) + lambda_init)
    first = lax.broadcasted_iota(jnp.int32, (1, LANES), 1) < DIFF_HEAD_DIM
    out_gain = sg_ref[...] * (1.0 - lambda_init)
    nt = (((1,), (1,)), ((), ()))
    visible = ((lax.broadcasted_iota(jnp.int32, (ATT_QBLK, 1), 0) // CHUNK)
               <= (lax.broadcasted_iota(jnp.int32, (1, 2 * ATT_QBLK), 1) % ATT_QBLK // CHUNK))

    def scores(item):
        hd, blk = item
        row0 = blk * ATT_QBLK
        klen = row0 + ATT_QBLK
        lanes = slice(hd * LANES, (hd + 1) * LANES)
        q = q_ref[row0:klen, lanes]
        q_both = jnp.concatenate([jnp.where(first, q, jnp.zeros_like(q)),
                                  jnp.where(first, jnp.zeros_like(q), q)], axis=0)
        return lax.dot_general(k_ref[0:klen, lanes], q_both, nt,
                               preferred_element_type=F32)

    items = [(hd, blk) for hd in range(ATT_HEADS_PER_STEP) for blk in range(SEQ // ATT_QBLK)]
    ahead = [scores(item) for item in items[:ATT_AHEAD]]
    for n, (hd, blk) in enumerate(items):
        row0 = blk * ATT_QBLK
        klen = row0 + ATT_QBLK
        s = ahead.pop(0)
        if n + ATT_AHEAD < len(items):
            ahead.append(scores(items[n + ATT_AHEAD]))
        s_diag = jnp.where(visible, s[row0:, :], -jnp.inf)
        m = jnp.max(s_diag, axis=0, keepdims=True)
        if row0:
            s_full = s[:row0, :]
            m = jnp.maximum(m, jnp.max(s_full, axis=0, keepdims=True))
        p = jnp.exp2(s_diag - m).astype(BF16)
        if row0:
            p = jnp.concatenate([jnp.exp2(s_full - m).astype(BF16), p], axis=0)
        pv = jnp.dot(vt_ext[hd * V_EXT:(hd + 1) * V_EXT, 0:klen], p,
                     preferred_element_type=F32)
        inv = 1.0 / pv[DIFF_V_DIM:DIFF_V_DIM + 1, :]
        o = (pv[:DIFF_V_DIM, :ATT_QBLK] * inv[:, :ATT_QBLK]
             - pv[:DIFF_V_DIM, ATT_QBLK:] * (lam * inv[:, ATT_QBLK:]))
        ms = jnp.mean(o * o, axis=0, keepdims=True)
        o_ref[row0:klen, hd * LANES:(hd + 1) * LANES] = (
            (o * lax.rsqrt(ms + EPS)).T * out_gain).astype(BF16)


def _attn_call(q, k, vt, diff_lambda, subln_g, layer):
    lambda_init = 0.8 - 0.6 * float(np.exp(-0.3 * layer))
    head = lambda: pl.BlockSpec((None, SEQ, ATT_HEADS_PER_STEP * LANES), lambda b, h: (b, 0, h))
    shp = (BATCH, SEQ, DIFF_WIDTH)
    return pl.pallas_call(
        functools.partial(_attn_kernel, lambda_init=lambda_init),
        grid=(BATCH, DIFF_HEADS // ATT_HEADS_PER_STEP),
        in_specs=[
            head(), head(),
            pl.BlockSpec((None, ATT_HEADS_PER_STEP * V_EXT, SEQ), lambda b, h: (b, h, 0)),
            _pick_spec(diff_lambda.shape, layer // 2),
            _pick_spec(subln_g.shape, layer // 2),
        ],
        out_specs=head(),
        out_shape=jax.ShapeDtypeStruct(shp, BF16),
        compiler_params=_params(("arbitrary", "arbitrary")),
        name=f"diff_attn_l{layer}",
    )(q.reshape(shp), k.reshape(shp), vt, diff_lambda, subln_g)


def _conv_out_stage(x_ref, gate_ref, a_ref, halo_ref, o_ref, cw_ref, cb_ref, lg_ref, lb_ref, w_ref,
                    abuf, shifted, conv, wrep):
    seq_start = (pl.program_id(0) % TILES_PER_SEQ) == 0
    halo = halo_ref[...]
    abuf[0:HALO, :] = jnp.where(seq_start, jnp.zeros_like(halo), halo)
    abuf[HALO:, :] = a_ref[...]
    lead = HALO - (CONV_WIDTH - 1)
    span = CONV_WIN + HALO - SUBLANES
    for j in range(CONV_WIDTH):
        wrep[j] = jnp.broadcast_to(cw_ref[j:j + 1, :], (SUBLANES, CONV_CH))
    bias = jnp.broadcast_to(cb_ref[...], (SUBLANES, CONV_CH))
    groups = CONV_ROWS // SUBLANES
    for base in range(0, TM, CONV_WIN):
        for r in range(1, SUBLANES):
            shifted[r - 1, 0:span, :] = abuf[base + r:base + r + span, :]
        for r0 in range(0, CONV_WIN, CONV_ROWS):
            accs = [bias] * groups
            for j in range(CONV_WIDTH):
                whole, r = divmod(lead + j, SUBLANES)
                w = wrep[j]
                for grp in range(groups):
                    start = r0 + (whole + grp) * SUBLANES
                    if r == 0:
                        src = abuf[base + start:base + start + SUBLANES, :]
                    else:
                        src = shifted[r - 1, start:start + SUBLANES, :]
                    accs[grp] = accs[grp] + w * src
            for grp in range(groups):
                row = base + r0 + grp * SUBLANES
                conv[row:row + SUBLANES, :] = accs[grp]
    c = conv[...]
    mu = jnp.mean(c, axis=-1, keepdims=True)
    d = c - mu
    var = jnp.mean(d * d, axis=-1, keepdims=True)
    a = _silu(d * lax.rsqrt(var + EPS) * lg_ref[...] + lb_ref[...]).astype(BF16)
    y = (jnp.dot(a, w_ref[0:CONV_CH, :], preferred_element_type=F32)
         + jnp.dot(o_ref[...], w_ref[CONV_CH:, :], preferred_element_type=F32))
    return x_ref[...] + gate_ref[...] * y


def _pool_stage(x_ref, xh_ref, shift_ref, scale_ref, gate_ref, g_ref, pw_ref, ps_ref, hbuf, level):
    t_in_seq = (pl.program_id(0) % TILES_PER_SEQ) * TM
    x = x_ref[...]
    g, shift, scale = g_ref[...], shift_ref[...], scale_ref[...]
    hh = _modulate(xh_ref[...], g, shift, scale)
    hbuf[0:POOL_HALO, :] = jnp.where(t_in_seq == 0, jnp.zeros_like(hh), hh)
    hbuf[POOL_HALO:, :] = _modulate(x, g, shift, scale)
    pos = (t_in_seq + 1 + lax.broadcasted_iota(jnp.int32, (TM, 1), 0)).astype(F32)
    rows = POOL_HALO + TM
    ys = []
    for grp, win in enumerate(POOL_WINDOWS):
        lo, hi = grp * POOL_GROUP_CH, (grp + 1) * POOL_GROUP_CH
        cur = hbuf[POOL_HALO:, lo:hi]
        src = lambda r0, r1: hbuf[r0:r1, lo:hi]
        lag, k = 1, 0
        while 2 * lag < win:
            start = SUBLANES * (k + 1)
            level[k, start:rows, :] = src(start, rows) + src(start - lag, rows - lag)
            src = lambda r0, r1, k=k: level[k, r0:r1, :]
            lag, k = 2 * lag, k + 1
        tot = src(POOL_HALO, rows) + src(POOL_HALO - lag, rows - lag)
        p = (tot / jnp.minimum(pos, float(win)) - cur).astype(BF16)
        ys.append(jnp.dot(p, pw_ref[grp], preferred_element_type=F32))
    y = jnp.concatenate(ys, axis=-1) * ps_ref[...]
    return x + gate_ref[...] * y


def _mixer_kernel(*refs, stage, n_in):
    out_ref = refs[n_in]
    out_ref[...] = stage(*refs[:n_in], *refs[n_in + 1:])


def _tok_spec(width):
    return pl.BlockSpec((TM, width), lambda i: (i, 0))


def _halo_spec(rows, width):
    return pl.BlockSpec((rows, width), lambda i: (jnp.maximum(i * (TM // rows) - 1, 0), 0))


def _mixer_call(stage, args, specs, scratch, name):
    return pl.pallas_call(
        functools.partial(_mixer_kernel, stage=stage, n_in=len(args)),
        grid=(N_TILES,),
        in_specs=specs,
        out_specs=_tok_spec(D_MODEL),
        out_shape=jax.ShapeDtypeStruct((TOKENS, D_MODEL), F32),
        scratch_shapes=scratch,
        compiler_params=_params(("arbitrary",)),
        name=name,
    )(*args)


def _conv_out_call(x, mods, a, o, conv_w, conv_b, ln_g, ln_b, w_out, layer):
    e = layer // 2
    specs = [
        _tok_spec(D_MODEL),
        _mod_spec(layer, 1, 2),
        _tok_spec(CONV_CH),
        _halo_spec(HALO, CONV_CH),
        _tok_spec(DIFF_WIDTH),
        _pick_spec(conv_w.shape, e),
        _pick_spec(conv_b.shape, e),
        _pick_spec(ln_g.shape, e),
        _pick_spec(ln_b.shape, e),
        _pick_spec(w_out.shape),
    ]
    scratch = [pltpu.VMEM((HALO + TM, CONV_CH), F32),
               pltpu.VMEM((SUBLANES - 1, CONV_WIN + HALO - SUBLANES, CONV_CH), F32),
               pltpu.VMEM((TM, CONV_CH), F32),
               pltpu.VMEM((CONV_WIDTH, SUBLANES, CONV_CH), F32)]
    return _mixer_call(_conv_out_stage, [x, mods, a, a, o, conv_w, conv_b, ln_g, ln_b, w_out],
                       specs, scratch, f"mix_out_l{layer}")


def _pool_call(x, mods, norm_g, pool_w, pool_scale, layer):
    specs = [
        _tok_spec(D_MODEL),
        _halo_spec(POOL_HALO, D_MODEL),
        _mod_spec(layer, 1, 0),
        _mod_spec(layer, 1, 1),
        _mod_spec(layer, 1, 2),
        _pick_spec(norm_g.shape, layer, 1),
        _pick_spec(pool_w.shape),
        _pick_spec(pool_scale.shape, layer // 2),
    ]
    n_levels = int(np.log2(max(POOL_WINDOWS))) - 1
    assert SUBLANES * (n_levels + 1) <= POOL_HALO and max(POOL_WINDOWS) <= POOL_HALO
    scratch = [pltpu.VMEM((POOL_HALO + TM, D_MODEL), F32),
               pltpu.VMEM((n_levels, POOL_HALO + TM, POOL_GROUP_CH), F32)]
    return _mixer_call(_pool_stage, [x, x, mods, mods, mods, norm_g, pool_w, pool_scale],
                       specs, scratch, f"pool_l{layer}")


def _ffn_kernel(x_ref, shift_ref, scale_ref, gate_ref, g_ref, w1_ref, w3_ref, w2_ref, fg_ref,
                *rest, final, n_round, chain):
    o_ref = rest[n_round]
    _round_slabs(rest[:n_round], rest[n_round + 1:])

    for r0 in range(0, TM_FFN, chain):
        x = x_ref[r0:r0 + chain, :]
        h = _modulate(x, g_ref[...], shift_ref[...], scale_ref[...]).astype(BF16)
        a = jnp.dot(h, w1_ref[...], preferred_element_type=F32)
        b = jnp.dot(h, w3_ref[...], preferred_element_type=F32)
        u = (_silu(a) * b).astype(BF16)
        y = jnp.dot(u, w2_ref[...], preferred_element_type=F32)
        xo = x + (0.5 * gate_ref[...]) * y
        if final:
            ms = jnp.mean(xo * xo, axis=-1, keepdims=True)
            xo = xo * lax.rsqrt(ms + EPS) * fg_ref[...]
        o_ref[r0:r0 + chain, :] = xo


def _ffn_call(x, mods, norm_g, weights, final_g, layer, sub, final, round_jobs):
    steps = TOKENS // TM_FFN
    after_light = (sub == 2 and not final) or (layer == 0 and sub == 0)
    chain = FFN_SUB_AFTER_LIGHT if after_light else FFN_SUB
    slab_in, slab_out, slab_shape = _round_specs(round_jobs, steps)
    in_specs = [
        pl.BlockSpec((TM_FFN, D_MODEL), lambda i: (i, 0)),
        _mod_spec(layer, sub, 0, TM_FFN),
        _mod_spec(layer, sub, 1, TM_FFN),
        _mod_spec(layer, sub, 2, TM_FFN),
        _pick_spec(norm_g.shape, layer, sub),
        _pick_spec(weights[0].shape),
        _pick_spec(weights[1].shape),
        _pick_spec(weights[2].shape),
        _pick_spec(final_g.shape),
    ]
    args = [x, mods, mods, mods, norm_g, *weights, final_g] + [w for w, _ in round_jobs]
    outs = pl.pallas_call(
        functools.partial(_ffn_kernel, final=final, n_round=len(round_jobs), chain=chain),
        grid=(steps,),
        in_specs=in_specs + slab_in,
        out_specs=[pl.BlockSpec((TM_FFN, D_MODEL), lambda i: (i, 0))] + slab_out,
        out_shape=[jax.ShapeDtypeStruct((TOKENS, D_MODEL), F32)] + slab_shape,
        compiler_params=_params(("arbitrary",)),
        name=f"ffn_l{layer}_s{sub}",
    )(*args)
    return outs[0], list(outs[1:])


def kernel(x, c, positions, w_ada, b_ada, norm_g, ffn_w1, ffn_w3, ffn_w2, w_in, w_out, conv_w, conv_b,
           conv_ln_g, conv_ln_b, diff_lambda, subln_g, pool_w, pool_scale, final_g):
    mods = _ada_call(c, w_ada, b_ada).reshape(DEPTH * MOD_ROWS_PER_LAYER, 1, D_MODEL)
    rows = lambda v: v.reshape(v.shape[:-1] + (1, v.shape[-1]))
    norm_g, final_g = rows(norm_g), rows(final_g)
    conv_b, conv_ln_g, conv_ln_b = rows(conv_b), rows(conv_ln_g), rows(conv_ln_b)
    subln_g, pool_scale = rows(subln_g), rows(pool_scale)
    pool_w2d = pool_w.reshape(pool_w.shape[0], len(POOL_WINDOWS) * POOL_GROUP_CH, POOL_GROUP_CH)

    ffn_jobs = lambda l, half: [(ffn_w1, (l, half)), (ffn_w3, (l, half)), (ffn_w2, (l, half))]
    tab, weights = _rot_call(positions, ffn_jobs(0, 0))

    xt = x.reshape(TOKENS, D_MODEL)
    for l in range(DEPTH):
        last = l == DEPTH - 1
        mixer_jobs = ([(w_in, (l // 2,)), (w_out, (l // 2,))] if l % 2 == 0
                      else [(pool_w2d, (l // 2,))])
        xt, rounded = _ffn_call(xt, mods, norm_g, weights, final_g, l, 0, False,
                                ffn_jobs(l, 1) + mixer_jobs)
        weights, mixer_w = rounded[:3], rounded[3:]
        if l % 2 == 0:
            w_in_b, w_out_b = mixer_w
            a, q, k, vt = _mix_in_call(xt, mods, norm_g, tab, w_in_b, l)
            o = _attn_call(q, k, vt, diff_lambda, subln_g, l)
            xt = _conv_out_call(xt, mods, a, o.reshape(TOKENS, DIFF_WIDTH), conv_w, conv_b,
                                conv_ln_g, conv_ln_b, w_out_b, l)
        else:
            pool_w_b = mixer_w[0].reshape(pool_w.shape[1:])
            xt = _pool_call(xt, mods, norm_g, pool_w_b, pool_scale, l)
        xt, weights = _ffn_call(xt, mods, norm_g, weights, final_g, l, 2, last,
                                [] if last else ffn_jobs(l + 1, 0))
    return xt.reshape(BATCH, SEQ, D_MODEL)
```
